```python
import math
import jax
import jax.numpy as jnp
from jax import lax
import numpy as np

D_MODEL = 1024
BATCH = 4
SEQ = 4096
DEPTH = 4
DEC_BATCH = 128
DEC_SEQ = 4
PAST_LEN = 8192
PAGE_SIZE = 128

N_HEADS = 16
HEAD_DIM = 64
N_A = (DEPTH + 1) // 2
N_B = DEPTH // 2

NSA_KV_HEADS = 4
NSA_GROUP = N_HEADS // NSA_KV_HEADS
CMP_STRIDE = 16
CMP_LEN = 2 * CMP_STRIDE
SEL_BLOCK = 64
N_SEL = 16
NSA_WINDOW = 512
NSA_QBLOCK = 64
NSA_IN = N_HEADS * HEAD_DIM + 6 * NSA_KV_HEADS * HEAD_DIM + 3 * N_HEADS

SWA_KV_HEADS = 2
SWA_GROUP = N_HEADS // SWA_KV_HEADS
SWA_WINDOW = 128
SWA_IN = N_HEADS * HEAD_DIM + 2 * SWA_KV_HEADS * HEAD_DIM

REL_BUCKETS = 32
REL_MAX_EXACT = 16
REL_MAX_DIST = 128

N_GROUPS = 4
EXPERTS_PER_GROUP = 8
N_EXPERTS = N_GROUPS * EXPERTS_PER_GROUP
TOP_K = 2
D_EXPERT = 512
MOE_BLOCK = 128

DN_ALPHA = (2 * DEPTH) ** 0.25
DN_BETA = (8 * DEPTH) ** -0.25
LN_EPS = 1e-5
SCALE = HEAD_DIM ** -0.5
NEG = -1e30
TINY = 1e-30
FORCED_SCORE = 1e4

kernel_name = 'hybrid_nsa_swa_sink_hier_moe_step'


def layer_norm(x, g, b):
    xf = x.astype(jnp.float32)
    mu = jnp.mean(xf, -1, keepdims=True)
    var = jnp.mean(jnp.square(xf - mu), -1, keepdims=True)
    return ((xf - mu) * lax.rsqrt(var + LN_EPS) * g + b).astype(x.dtype)


def masked_softmax(logits, mask, axis):
    l = jnp.where(mask, logits, NEG)
    m = jnp.max(l, axis, keepdims=True)
    e = jnp.where(mask, jnp.exp(l - m), 0.0)
    return e / jnp.maximum(jnp.sum(e, axis, keepdims=True), TINY)


def rel_bucket(dist):
    n = jnp.maximum(dist, 0)
    nf = jnp.maximum(n, 1).astype(jnp.float32)
    far = REL_MAX_EXACT + (jnp.log(nf / REL_MAX_EXACT) / math.log(REL_MAX_DIST / REL_MAX_EXACT)
                           * (REL_BUCKETS - REL_MAX_EXACT)).astype(jnp.int32)
    return jnp.where(n < REL_MAX_EXACT, n, jnp.minimum(far, REL_BUCKETS - 1))


def nsa_project(x, w_in, b_in):
    lead = x.shape[:-1]
    hq = N_HEADS * HEAD_DIM
    hkv = 6 * NSA_KV_HEADS * HEAD_DIM
    h = x @ w_in + b_in
    q = h[..., :hq].reshape(*lead, NSA_KV_HEADS, NSA_GROUP, HEAD_DIM)
    kv = h[..., hq:hq + hkv].reshape(*lead, 6, NSA_KV_HEADS, HEAD_DIM)
    gates = jax.nn.sigmoid(h[..., hq + hkv:].astype(jnp.float32)).astype(x.dtype)
    gates = gates.reshape(*lead, NSA_KV_HEADS, NSA_GROUP, 3)
    return q, kv, gates


def chunk_pq(rows, w1):
    b, l = rows.shape[:2]
    ch = rows.reshape(b, l // CMP_STRIDE, CMP_STRIDE, NSA_KV_HEADS, HEAD_DIM)
    p = jnp.einsum('bnjgd,jde->bnge', ch, w1[:CMP_STRIDE])
    qq = jnp.einsum('bnjgd,jde->bnge', ch, w1[CMP_STRIDE:])
    return p, qq


def compress(p, qq, b1, w2):
    h = jax.nn.gelu(p[:, :-1] + qq[:, 1:] + b1)
    return jnp.einsum('bnge,ed->bngd', h, w2)


def nsa_attend(q, q_pos, k_cmp, v_cmp, gather_sel, n_sel_blocks, k_win, v_win, win_pos, gates, rel_table):
    b, nq = q.shape[0], q.shape[1]
    nc = k_cmp.shape[1]
    tbl = rel_table.reshape(REL_BUCKETS, NSA_KV_HEADS, NSA_GROUP)
    cmp_start = CMP_STRIDE * jnp.arange(nc)
    cmp_mask = (cmp_start + CMP_LEN - 1)[None, :] <= q_pos[:, None]
    s_c = jnp.einsum('bqghd,bngd->bghqn', q, k_cmp).astype(jnp.float32) * SCALE
    p_c = masked_softmax(s_c, cmp_mask, -1)
    o_c = jnp.einsum('bghqn,bngd->bqghd', p_c.astype(v_cmp.dtype), v_cmp)
    blk = jnp.arange(n_sel_blocks)
    overlap = ((cmp_start[:, None] < SEL_BLOCK * (blk[None, :] + 1))
               & ((cmp_start + CMP_LEN)[:, None] > SEL_BLOCK * blk[None, :])).astype(jnp.float32)
    imp = jnp.einsum('bghqn,nj->bgqj', p_c, overlap)
    cur = (q_pos // SEL_BLOCK)[:, None]
    visible = blk[None, :] <= cur
    forced = (blk[None, :] == 0) | (blk[None, :] == cur) | (blk[None, :] == cur - 1)
    score = jnp.where(visible, jnp.where(forced, FORCED_SCORE, imp), -1.0)
    top_s, idx = lax.top_k(score, min(N_SEL, n_sel_blocks))
    n_top = idx.shape[-1]
    k_s, v_s = gather_sel(idx)
    pos_s = idx[..., None] * SEL_BLOCK + jnp.arange(SEL_BLOCK)
    dist_s = q_pos[None, None, :, None, None] - pos_s
    mask_s = (top_s > -0.5)[..., None] & (dist_s >= 0)
    g_idx = jnp.arange(NSA_KV_HEADS)[None, :, None, None, None]
    s_s = (jnp.einsum('bqghd,bgqnsd->bgqnsh', q, k_s).astype(jnp.float32) * SCALE
           + tbl[rel_bucket(dist_s), g_idx].astype(jnp.float32))
    s_s = s_s.reshape(b, NSA_KV_HEADS, nq, n_top * SEL_BLOCK, NSA_GROUP)
    p_s = masked_softmax(s_s, mask_s.reshape(b, NSA_KV_HEADS, nq, n_top * SEL_BLOCK, 1), -2)
    o_s = jnp.einsum('bgqkh,bgqkd->bqghd', p_s.astype(v_s.dtype),
                     v_s.reshape(b, NSA_KV_HEADS, nq, n_top * SEL_BLOCK, HEAD_DIM))
    dist_w = q_pos[:, None] - win_pos[None, :]
    mask_w = (dist_w >= 0) & (dist_w < NSA_WINDOW) & (win_pos >= 0)[None, :]
    bias_w = jnp.transpose(tbl[rel_bucket(dist_w)], (2, 3, 0, 1)).astype(jnp.float32)
    s_w = jnp.einsum('bqghd,bkgd->bghqk', q, k_win).astype(jnp.float32) * SCALE + bias_w
    p_w = masked_softmax(s_w, mask_w, -1)
    o_w = jnp.einsum('bghqk,bkgd->bqghd', p_w.astype(v_win.dtype), v_win)
    return gates[..., 0:1] * o_c + gates[..., 1:2] * o_s + gates[..., 2:3] * o_w


def nsa_prompt(x, w_in, b_in, w1, b1, w2, w_out, rel_table):
    b, l, _ = x.shape
    q, kv, gates = nsa_project(x, w_in, b_in)
    k_cmp = compress(*chunk_pq(kv[:, :, 0], w1[0]), b1[0], w2[0])
    v_cmp = compress(*chunk_pq(kv[:, :, 1], w1[1]), b1[1], w2[1])
    n_blk = l // SEL_BLOCK

    def to_blocks(r):
        return r.reshape(b, n_blk, SEL_BLOCK, NSA_KV_HEADS, HEAD_DIM).transpose(0, 3, 1, 2, 4)

    ks_b, vs_b = to_blocks(kv[:, :, 2]), to_blocks(kv[:, :, 3])
    b_idx = jnp.arange(b)[:, None, None, None]
    g_idx = jnp.arange(NSA_KV_HEADS)[None, :, None, None]

    def gather_sel(idx):
        return ks_b[b_idx, g_idx, idx], vs_b[b_idx, g_idx, idx]

    pad = jnp.zeros((b, NSA_WINDOW, 2, NSA_KV_HEADS, HEAD_DIM), x.dtype)
    win = jnp.concatenate([pad, kv[:, :, 4:6]], axis=1)

    def query_block(i):
        start = i * NSA_QBLOCK
        q_pos = start + jnp.arange(NSA_QBLOCK)
        qb = lax.dynamic_slice_in_dim(q, start, NSA_QBLOCK, axis=1)
        gb = lax.dynamic_slice_in_dim(gates, start, NSA_QBLOCK, axis=1)
        wb = lax.dynamic_slice_in_dim(win, start, NSA_WINDOW + NSA_QBLOCK, axis=1)
        win_pos = start - NSA_WINDOW + jnp.arange(NSA_WINDOW + NSA_QBLOCK)
        return nsa_attend(qb, q_pos, k_cmp, v_cmp, gather_sel, n_blk,
                          wb[:, :, 0], wb[:, :, 1], win_pos, gb, rel_table)

    o = lax.map(query_block, jnp.arange(l // NSA_QBLOCK))
    y = jnp.moveaxis(o, 0, 1).reshape(b, l, N_HEADS * HEAD_DIM) @ w_out
    wl = min(NSA_WINDOW, l)
    return y, kv[:, :, 0:2], kv[:, :, 2:4], kv[:, l - wl:, 4:6]


def nsa_sample(x, cache_cmp, cache_sel, win_buf, page_table, a, w_in, b_in, w1, b1, w2, w_out, rel_table):
    b, t, _ = x.shape
    n_pages = page_table.shape[1]
    past = n_pages * PAGE_SIZE
    q, kv, gates = nsa_project(x, w_in, b_in)
    t_pad = -(-t // SEL_BLOCK) * SEL_BLOCK
    tail = jnp.pad(kv, ((0, 0), (0, t_pad - t), (0, 0), (0, 0), (0, 0)))

    def page_pq(phys):
        rows = cache_cmp[a, phys]
        pk, qk = chunk_pq(rows[:, :, 0], w1[0])
        pv, qv = chunk_pq(rows[:, :, 1], w1[1])
        return pk, qk, pv, qv

    pk, qk, pv, qv = [jnp.moveaxis(z, 0, 1).reshape(b, past // CMP_STRIDE, NSA_KV_HEADS, HEAD_DIM)
                      for z in lax.map(page_pq, page_table.T)]
    tpk, tqk = chunk_pq(tail[:, :, 0], w1[0])
    tpv, tqv = chunk_pq(tail[:, :, 1], w1[1])
    k_cmp = compress(jnp.concatenate([pk, tpk], 1), jnp.concatenate([qk, tqk], 1), b1[0], w2[0])
    v_cmp = compress(jnp.concatenate([pv, tpv], 1), jnp.concatenate([qv, tqv], 1), b1[1], w2[1])

    n_past_blk = past // SEL_BLOCK
    blk_per_page = PAGE_SIZE // SEL_BLOCK
    n_blk = n_past_blk + t_pad // SEL_BLOCK
    b_idx = jnp.arange(b)[:, None, None, None, None]
    g_idx = jnp.arange(NSA_KV_HEADS)[None, :, None, None, None]
    s_off = jnp.arange(SEL_BLOCK)

    def gather_sel(idx):
        j = idx[..., None]
        in_past = (j < n_past_blk)[..., None]
        phys = page_table[b_idx, jnp.minimum(j // blk_per_page, n_pages - 1)]
        row = (j % blk_per_page) * SEL_BLOCK + s_off
        trow = jnp.clip((j - n_past_blk) * SEL_BLOCK + s_off, 0, t_pad - 1)
        k = jnp.where(in_past, cache_sel[a, phys, row, 0, g_idx], tail[b_idx, trow, 2, g_idx])
        v = jnp.where(in_past, cache_sel[a, phys, row, 1, g_idx], tail[b_idx, trow, 3, g_idx])
        return k, v

    wl = win_buf.shape[1]
    win = jnp.concatenate([win_buf, kv[:, :, 4:6]], axis=1)
    win_pos = past - wl + jnp.arange(wl + t)
    q_pos = past + jnp.arange(t)
    o = nsa_attend(q, q_pos, k_cmp, v_cmp, gather_sel, n_blk,
                   win[:, :, 0], win[:, :, 1], win_pos, gates, rel_table)
    y = o.reshape(b, t, N_HEADS * HEAD_DIM) @ w_out
    return y, kv[:, :, 0:2], kv[:, :, 2:4], win[:, t:]


def swa_project(x, w_in):
    lead = x.shape[:-1]
    hq = N_HEADS * HEAD_DIM
    hk = SWA_KV_HEADS * HEAD_DIM
    h = x @ w_in
    q = h[..., :hq].reshape(*lead, SWA_KV_HEADS, SWA_GROUP, HEAD_DIM)
    k = h[..., hq:hq + hk].reshape(*lead, SWA_KV_HEADS, HEAD_DIM)
    v = h[..., hq + hk:].reshape(*lead, SWA_KV_HEADS, HEAD_DIM)
    return q, k, v


def swa_attend(q, q_pos, k, v, k_pos, sinks, rel_table):
    tbl = rel_table.reshape(REL_BUCKETS, SWA_KV_HEADS, SWA_GROUP)
    dist = q_pos[:, :, None] - k_pos[:, None, :]
    mask = (dist >= 0) & (dist < SWA_WINDOW) & (k_pos[:, None, :] >= 0)
    bias = jnp.transpose(tbl[rel_bucket(dist)], (0, 3, 4, 1, 2)).astype(jnp.float32)
    s = jnp.einsum('bnqghd,bnkgd->bnghqk', q, k).astype(jnp.float32) * SCALE + bias
    s = jnp.where(mask[None, :, None, None], s, NEG)
    sink = sinks.astype(jnp.float32).reshape(SWA_KV_HEADS, SWA_GROUP)[None, None, :, :, None, None]
    m = jnp.maximum(jnp.max(s, -1, keepdims=True), sink)
    e = jnp.exp(s - m)
    p = e / (jnp.sum(e, -1, keepdims=True) + jnp.exp(sink - m))
    return jnp.einsum('bnghqk,bnkgd->bnqghd', p.astype(v.dtype), v)


def swa_prompt(x, w_in, sinks, w_out, rel_table):
    b, l, _ = x.shape
    q, k, v = swa_project(x, w_in)
    nb = l // SWA_WINDOW

    def band(r):
        rp = jnp.concatenate([jnp.zeros_like(r[:, :SWA_WINDOW]), r], axis=1)
        rp = rp.reshape(b, nb + 1, SWA_WINDOW, SWA_KV_HEADS, HEAD_DIM)
        return jnp.concatenate([rp[:, :-1], rp[:, 1:]], axis=2)

    q_pos = jnp.arange(l).reshape(nb, SWA_WINDOW)
    k_pos = (jnp.arange(nb) * SWA_WINDOW - SWA_WINDOW)[:, None] + jnp.arange(2 * SWA_WINDOW)[None, :]
    qb = q.reshape(b, nb, SWA_WINDOW, SWA_KV_HEADS, SWA_GROUP, HEAD_DIM)
    o = swa_attend(qb, q_pos, band(k), band(v), k_pos, sinks, rel_table)
    y = o.reshape(b, l, N_HEADS * HEAD_DIM) @ w_out
    wl = min(SWA_WINDOW, l)
    return y, jnp.stack([k[:, l - wl:], v[:, l - wl:]], axis=2)


def swa_sample(x, buf, past, w_in, sinks, w_out, rel_table):
    b, t, _ = x.shape
    wl = buf.shape[1]
    q, k, v = swa_project(x, w_in)
    rows = jnp.concatenate([buf, jnp.stack([k, v], axis=2)], axis=1)
    k_pos = past - wl + jnp.arange(wl + t)
    q_pos = past + jnp.arange(t)
    o = swa_attend(q[:, None], q_pos[None], rows[:, None, :, 0], rows[:, None, :, 1], k_pos[None], sinks, rel_table)
    y = o[:, 0].reshape(b, t, N_HEADS * HEAD_DIM) @ w_out
    return y, rows[:, t:]


def grouped_expert_mlp(xt, ids, w_gate, w_up, w_down):
    n = ids.shape[0]
    order = jnp.argsort(ids)
    sid = ids[order]
    counts = jnp.bincount(ids, length=N_EXPERTS)
    padded = (counts + MOE_BLOCK - 1) // MOE_BLOCK * MOE_BLOCK
    pad_end = jnp.cumsum(padded)
    pad_start = pad_end - padded
    raw_start = jnp.cumsum(counts) - counts
    dest = pad_start[sid] + jnp.arange(n) - raw_start[sid]
    n_blocks = -(-n // MOE_BLOCK) + N_EXPERTS
    buf = jnp.zeros((n_blocks * MOE_BLOCK, D_MODEL), xt.dtype).at[dest].set(xt[order // TOP_K])
    blk_start = jnp.arange(n_blocks) * MOE_BLOCK
    blk_expert = jnp.minimum(jnp.sum(pad_end[None, :] <= blk_start[:, None], axis=1), N_EXPERTS - 1)

    def run(args):
        xb, e = args
        return (jax.nn.silu(xb @ w_gate[e]) * (xb @ w_up[e])) @ w_down[e]

    out = lax.map(run, (buf.reshape(n_blocks, MOE_BLOCK, D_MODEL), blk_expert)).reshape(-1, D_MODEL)
    return jnp.zeros((n, D_MODEL), out.dtype).at[order].set(out[dest])


def moe_ffn(x, w_grp, b_grp, w_exp, b_exp, w_gate, w_up, w_down):
    lead = x.shape[:-1]
    xt = x.reshape(-1, D_MODEL)
    n_tok = xt.shape[0]
    g_prob = jax.nn.softmax((xt @ w_grp + b_grp).astype(jnp.float32), -1)
    g_p, g_i = lax.top_k(g_prob, 1)
    e_logits = (xt @ w_exp + b_exp).astype(jnp.float32).reshape(n_tok, N_GROUPS, EXPERTS_PER_GROUP)
    e_logits = e_logits[jnp.arange(n_tok), g_i[:, 0]]
    e_p, e_i = lax.top_k(jax.nn.softmax(e_logits, -1), TOP_K)
    w = e_p / jnp.sum(e_p, -1, keepdims=True) * g_p
    ids = (g_i * EXPERTS_PER_GROUP + e_i).reshape(-1)
    out = grouped_expert_mlp(xt, ids, w_gate, w_up, w_down).reshape(n_tok, TOP_K, D_MODEL)
    y = jnp.einsum('tkd,tk->td', out, w.astype(out.dtype))
    return y.reshape(*lead, D_MODEL)


def setup_inputs(seed: int = 0) -> dict:
    key = jax.random.key(seed)
    ks = jax.random.split(key, 32)
    f32 = jnp.float32

    def nrm(i, shape, scale):
        return jax.random.normal(ks[i], shape, f32) * scale

    n_pages = PAST_LEN // PAGE_SIZE
    n_phys = (DEC_BATCH * n_pages * 5) // 4
    page_table = jax.random.permutation(ks[0], n_phys)[:DEC_BATCH * n_pages]
    page_table = page_table.reshape(DEC_BATCH, n_pages).astype(jnp.int32)
    pool = (N_A, n_phys, PAGE_SIZE, 2, NSA_KV_HEADS, HEAD_DIM)
    hqd = N_HEADS * HEAD_DIM
    return {
        'x_prompt': nrm(1, (BATCH, SEQ, D_MODEL), 1.0),
        'x_sample': nrm(2, (DEC_BATCH, DEC_SEQ, D_MODEL), 1.0),
        'cache_nsa_cmp': nrm(3, pool, 1.0),
        'cache_nsa_sel': nrm(4, pool, 1.0),
        'state_nsa_win': nrm(5, (N_A, DEC_BATCH, min(NSA_WINDOW, PAST_LEN), 2, NSA_KV_HEADS, HEAD_DIM), 1.0),
        'state_swa_win': nrm(6, (N_B, DEC_BATCH, min(SWA_WINDOW, PAST_LEN), 2, SWA_KV_HEADS, HEAD_DIM), 1.0),
        'page_table': page_table,
        'rel_bias': nrm(7, (REL_BUCKETS, N_HEADS), 0.5),
        'nsa_w_in': nrm(8, (N_A, D_MODEL, NSA_IN), D_MODEL ** -0.5),
        'nsa_b_in': nrm(9, (N_A, NSA_IN), 0.02),
        'nsa_cmp_w1': nrm(10, (N_A, 2, CMP_LEN, HEAD_DIM, HEAD_DIM), (CMP_LEN * HEAD_DIM) ** -0.5),
        'nsa_cmp_b1': nrm(11, (N_A, 2, HEAD_DIM), 0.02),
        'nsa_cmp_w2': nrm(12, (N_A, 2, HEAD_DIM, HEAD_DIM), HEAD_DIM ** -0.5),
        'nsa_w_out': nrm(13, (N_A, hqd, D_MODEL), hqd ** -0.5 * DN_BETA),
        'swa_w_in': nrm(14, (N_B, D_MODEL, SWA_IN), D_MODEL ** -0.5),
        'swa_sinks': nrm(15, (N_B, N_HEADS), 0.5),
        'swa_w_out': nrm(16, (N_B, hqd, D_MODEL), hqd ** -0.5 * DN_BETA),
        'moe_w_group': nrm(17, (DEPTH, D_MODEL, N_GROUPS), D_MODEL ** -0.5),
        'moe_b_group': nrm(18, (DEPTH, N_GROUPS), 0.01),
        'moe_w_expert': nrm(19, (DEPTH, D_MODEL, N_EXPERTS), D_MODEL ** -0.5),
        'moe_b_expert': nrm(20, (DEPTH, N_EXPERTS), 0.01),
        'moe_w_gate': nrm(21, (DEPTH, N_EXPERTS, D_MODEL, D_EXPERT), D_MODEL ** -0.5),
        'moe_w_up': nrm(22, (DEPTH, N_EXPERTS, D_MODEL, D_EXPERT), D_MODEL ** -0.5),
        'moe_w_down': nrm(23, (DEPTH, N_EXPERTS, D_EXPERT, D_MODEL), D_EXPERT ** -0.5 * DN_BETA),
        'ln_g': 1.0 + nrm(24, (DEPTH, 2, D_MODEL), 0.05),
        'ln_b': nrm(25, (DEPTH, 2, D_MODEL), 0.02),
    }


def reference(x_prompt, x_sample, cache_nsa_cmp, cache_nsa_sel, state_nsa_win, state_swa_win, page_table,
              rel_bias, nsa_w_in, nsa_b_in, nsa_cmp_w1, nsa_cmp_b1, nsa_cmp_w2, nsa_w_out,
              swa_w_in, swa_sinks, swa_w_out, moe_w_group, moe_b_group, moe_w_expert, moe_b_expert,
              moe_w_gate, moe_w_up, moe_w_down, ln_g, ln_b):
    past = page_table.shape[1] * PAGE_SIZE
    hp, hs = x_prompt, x_sample
    cmp_p, cmp_s, sel_p, sel_s = [], [], [], []
    nwin_p, nwin_s, swin_p, swin_s = [], [], [], []
    for layer in range(DEPTH):
        j = layer // 2
        if layer % 2 == 0:
            mp, rc, rs, rw = nsa_prompt(hp, nsa_w_in[j], nsa_b_in[j], nsa_cmp_w1[j], nsa_cmp_b1[j],
                                        nsa_cmp_w2[j], nsa_w_out[j], rel_bias)
            ms, rc2, rs2, rw2 = nsa_sample(hs, cache_nsa_cmp, cache_nsa_sel, state_nsa_win[j], page_table, j,
                                           nsa_w_in[j], nsa_b_in[j], nsa_cmp_w1[j], nsa_cmp_b1[j],
                                           nsa_cmp_w2[j], nsa_w_out[j], rel_bias)
            cmp_p.append(rc)
            sel_p.append(rs)
            nwin_p.append(rw)
            cmp_s.append(rc2)
            sel_s.append(rs2)
            nwin_s.append(rw2)
        else:
            mp, wp = swa_prompt(hp, swa_w_in[j], swa_sinks[j], swa_w_out[j], rel_bias)
            ms, ws = swa_sample(hs, state_swa_win[j], past, swa_w_in[j], swa_sinks[j], swa_w_out[j], rel_bias)
            swin_p.append(wp)
            swin_s.append(ws)
        hp = layer_norm(DN_ALPHA * hp + mp, ln_g[layer, 0], ln_b[layer, 0])
        hs = layer_norm(DN_ALPHA * hs + ms, ln_g[layer, 0], ln_b[layer, 0])
        moe_w = (moe_w_group[layer], moe_b_group[layer], moe_w_expert[layer], moe_b_expert[layer],
                 moe_w_gate[layer], moe_w_up[layer], moe_w_down[layer])
        hp = layer_norm(DN_ALPHA * hp + moe_ffn(hp, *moe_w), ln_g[layer, 1], ln_b[layer, 1])
        hs = layer_norm(DN_ALPHA * hs + moe_ffn(hs, *moe_w), ln_g[layer, 1], ln_b[layer, 1])
    new_cmp_prompt = jnp.stack(cmp_p)
    new_cmp_sample = jnp.stack(cmp_s)
    new_sel_prompt = jnp.stack(sel_p)
    new_sel_sample = jnp.stack(sel_s)
    new_nsa_win_prompt = jnp.stack(nwin_p)
    new_nsa_win_sample = jnp.stack(nwin_s)
    new_swa_win_prompt = jnp.stack(swin_p)
    new_swa_win_sample = jnp.stack(swin_s)
    return (hp, hs, new_cmp_prompt, new_cmp_sample, new_sel_prompt, new_sel_sample,
            new_nsa_win_prompt, new_nsa_win_sample, new_swa_win_prompt, new_swa_win_sample)
```

```python
import functools
import math

import jax
import jax.numpy as jnp
from jax import lax
from jax.experimental import pallas as pl
from jax.experimental.pallas import tpu as pltpu

F32 = jnp.float32
BF16 = jnp.bfloat16
I32 = jnp.int32

D_MODEL = 1024
N_HEADS = 16
HEAD_DIM = 64
NSA_KV = 4
NSA_G = N_HEADS // NSA_KV
CMP_STRIDE = 16
CMP_LEN = 32
SEL_BLOCK = 64
N_SEL = 16
NSA_WINDOW = 512
SWA_KV = 2
SWA_G = N_HEADS // SWA_KV
SWA_WINDOW = 128
REL_BUCKETS = 32
REL_MAX_EXACT = 16
REL_MAX_DIST = 128
N_GROUPS = 4
EPG = 8
N_EXPERTS = N_GROUPS * EPG
TOP_K = 2
D_EXPERT = 512
PAGE = 128
DEPTH = 4
DN_ALPHA = (2 * DEPTH) ** 0.25
LN_EPS = 1e-5
SCALE = HEAD_DIM ** -0.5
NEG = -1e30
NEG_FLOOR = -1e29
TINY = 1e-30
FORCED = 1e4

LANE = 128
TQ = 256
TM = 256
ROUTER_PAD = LANE
NPG_MAX = 16
VMEM_LIMIT = 52 * 1024 * 1024


def _cparams(sem):
    return pltpu.CompilerParams(dimension_semantics=sem, vmem_limit_bytes=VMEM_LIMIT)


def _dot_nt(a, b):
    return lax.dot_general(a, b, (((1,), (1,)), ((), ())), preferred_element_type=F32)


def _dot(a, b):
    return jnp.dot(a, b, preferred_element_type=F32)


def _split_bf16(x):
    hi = x.astype(BF16)
    lo = (x - hi.astype(F32)).astype(BF16)
    return hi, lo


def _proj_kernel(x_ref, w_ref, b_ref, o_ref):
    o_ref[...] = _dot(x_ref[...].astype(BF16), w_ref[...]) + b_ref[...]


def _project(x, w_bf16, bias):
    m, k = x.shape
    n = w_bf16.shape[1]
    return pl.pallas_call(
        _proj_kernel,
        grid=(m // TM,),
        in_specs=[pl.BlockSpec((TM, k), lambda i: (i, 0)),
                  pl.BlockSpec((k, n), lambda i: (0, 0)),
                  pl.BlockSpec((1, n), lambda i: (0, 0))],
        out_specs=pl.BlockSpec((TM, n), lambda i: (i, 0)),
        out_shape=jax.ShapeDtypeStruct((m, n), F32),
        compiler_params=_cparams(("parallel",)),
        name="project",
    )(x, w_bf16, bias)


def _layer_norm(z, g, b):
    mu = jnp.mean(z, -1, keepdims=True)
    zc = z - mu
    var = jnp.mean(zc * zc, -1, keepdims=True)
    return zc * lax.rsqrt(var + LN_EPS) * g + b


def _outproj_ln_router_kernel(o_ref, w_ref, x_ref, g_ref, b_ref, wrh_ref, wrl_ref, br_ref,
                              hn_ref, hb_ref, lg_ref):
    y = _dot(o_ref[...], w_ref[...])
    hn = _layer_norm(DN_ALPHA * x_ref[...] + y, g_ref[...], b_ref[...])
    hn_ref[...] = hn
    hi, lo = _split_bf16(hn)
    hb_ref[...] = hi
    lg_ref[...] = (_dot(hi, wrh_ref[...]) + _dot(lo, wrh_ref[...]) + _dot(hi, wrl_ref[...])) + br_ref[...]


def _outproj_ln_router(o_bf16, w_bf16, x, g, b, wr_hi, wr_lo, br):
    m, k = o_bf16.shape
    d = w_bf16.shape[1]
    row = lambda i: (i, 0)
    fix = lambda i: (0, 0)
    return pl.pallas_call(
        _outproj_ln_router_kernel,
        grid=(m // TM,),
        in_specs=[pl.BlockSpec((TM, k), row), pl.BlockSpec((k, d), fix), pl.BlockSpec((TM, d), row),
                  pl.BlockSpec((1, d), fix), pl.BlockSpec((1, d), fix),
                  pl.BlockSpec((d, ROUTER_PAD), fix), pl.BlockSpec((d, ROUTER_PAD), fix),
                  pl.BlockSpec((1, ROUTER_PAD), fix)],
        out_specs=[pl.BlockSpec((TM, d), row), pl.BlockSpec((TM, d), row), pl.BlockSpec((TM, ROUTER_PAD), row)],
        out_shape=[jax.ShapeDtypeStruct((m, d), F32), jax.ShapeDtypeStruct((m, d), BF16),
                   jax.ShapeDtypeStruct((m, ROUTER_PAD), F32)],
        compiler_params=_cparams(("parallel",)),
        name="outproj_ln_router",
    )(o_bf16, w_bf16, x, g, b, wr_hi, wr_lo, br)


def _experts_kernel(be_ref, nu_ref, x_ref, wg_ref, wu_ref, wd_ref, o_ref, wg_s, wu_s, wd_s):
    i = pl.program_id(0)
    prev = be_ref[jnp.maximum(i - 1, 0)]

    @pl.when((i == 0) | (be_ref[i] != prev))
    def _():
        wg_s[...] = wg_ref[...].astype(BF16)
        wu_s[...] = wu_ref[...].astype(BF16)
        wd_s[...] = wd_ref[...].astype(BF16)

    @pl.when(i < nu_ref[0])
    def _():
        x = x_ref[...]
        a = _dot(x, wg_s[...])
        u = _dot(x, wu_s[...])
        hid = (a * jax.nn.sigmoid(a)) * u
        o_ref[...] = _dot(hid.astype(BF16), wd_s[...])

    @pl.when(i >= nu_ref[0])
    def _():
        o_ref[...] = jnp.zeros_like(o_ref)


def _experts(x_sorted, blk_expert, n_used, w_gate, w_up, w_down):
    n_rows, d = x_sorted.shape
    n_blocks = n_rows // TM
    de = w_gate.shape[-1]
    wsel = lambda i, be, nu: (be[i], 0, 0)
    return pl.pallas_call(
        _experts_kernel,
        grid_spec=pltpu.PrefetchScalarGridSpec(
            num_scalar_prefetch=2,
            grid=(n_blocks,),
            in_specs=[pl.BlockSpec((TM, d), lambda i, be, nu: (i, 0)),
                      pl.BlockSpec((None, d, de), wsel),
                      pl.BlockSpec((None, d, de), wsel),
                      pl.BlockSpec((None, de, d), wsel)],
            out_specs=pl.BlockSpec((TM, d), lambda i, be, nu: (i, 0)),
            scratch_shapes=[pltpu.VMEM((d, de), BF16), pltpu.VMEM((d, de), BF16), pltpu.VMEM((de, d), BF16)]),
        out_shape=jax.ShapeDtypeStruct((n_rows, d), F32),
        compiler_params=_cparams(("arbitrary",)),
        name="experts",
    )(blk_expert, n_used, x_sorted, w_gate, w_up, w_down)


def _combine_ln_kernel(h_ref, o0_ref, o1_ref, w_ref, g_ref, b_ref, y_ref):
    w = w_ref[...]
    z = DN_ALPHA * h_ref[...] + (w[:, 0:1] * o0_ref[...] + w[:, 1:2] * o1_ref[...])
    y_ref[...] = _layer_norm(z, g_ref[...], b_ref[...])


def _combine_ln(h, o0, o1, w, g, b):
    m, d = h.shape
    row = lambda i: (i, 0)
    fix = lambda i: (0, 0)
    return pl.pallas_call(
        _combine_ln_kernel,
        grid=(m // TM,),
        in_specs=[pl.BlockSpec((TM, d), row), pl.BlockSpec((TM, d), row), pl.BlockSpec((TM, d), row),
                  pl.BlockSpec((TM, TOP_K), row), pl.BlockSpec((1, d), fix), pl.BlockSpec((1, d), fix)],
        out_specs=pl.BlockSpec((TM, d), row),
        out_shape=jax.ShapeDtypeStruct((m, d), F32),
        compiler_params=_cparams(("parallel",)),
        name="combine_ln",
    )(h, o0, o1, w, g, b)


def _online_value(q, k, v, add, state, feat_major=False):
    m_old, l_old, acc = state
    s = (_dot(q, k) if feat_major else _dot_nt(q, k)) + add
    m_new = jnp.maximum(m_old, jnp.max(s, -1, keepdims=True))
    m_use = jnp.maximum(m_new, NEG_FLOOR)
    alpha = jnp.exp(m_old - m_use)
    p = jnp.exp(s - m_use)
    pb = p.astype(BF16)
    pv = _dot_nt(pb, v) if feat_major else _dot(pb, v)
    return (m_new, alpha * l_old + jnp.sum(p, -1, keepdims=True), alpha * acc + pv)


def _online_step(q, k, v, add, m_ref, l_ref, acc_ref, idx, feat_major=False):
    m_ref[idx], l_ref[idx], acc_ref[idx] = _online_value(
        q, k, v, add, (m_ref[idx], l_ref[idx], acc_ref[idx]), feat_major)


def _masked_softmax(s, mask):
    l = jnp.where(mask, s, NEG)
    m = jnp.max(l, -1, keepdims=True)
    e = jnp.where(mask, jnp.exp(l - m), 0.0)
    return e * (1.0 / jnp.maximum(jnp.sum(e, -1, keepdims=True), TINY))


def _topk_mask(score, blk, n_blk):
    rank = jnp.zeros(score.shape, F32)
    for j in range(n_blk):
        col = score[:, j:j + 1]
        beats = (col > score) | ((col == score) & (j < blk))
        rank = rank + jnp.where(beats, 1.0, 0.0)
    return rank < float(N_SEL)


def _nsa_prompt_kernel(far_ref, q_ref, kc_ref, vc_ref, ks_ref, vs_ref, kw_ref, vw_ref, g_ref, b0_ref, b1_ref,
                       ex_ref, ov_ref, o_ref, msk_ref, oc_ref, ms_ref, ls_ref, as_ref, mw_ref, lw_ref, aw_ref,
                       *, n_tiles, n_cmp, n_blk):
    g = pl.program_id(1)
    t = pl.program_id(2)
    row = lax.broadcasted_iota(I32, (TQ, 1), 0)
    q_pos = t * TQ + row

    kc = kc_ref[0, 0]
    vc = vc_ref[0, 0]
    ncp = kc.shape[0]
    n_idx = lax.broadcasted_iota(I32, (1, ncp), 1)
    cmask = (CMP_STRIDE * n_idx + (CMP_LEN - 1) <= q_pos) & (n_idx < n_cmp)
    p_sum = jnp.zeros((TQ, ncp), F32)
    for h in range(NSA_G):
        p = _masked_softmax(_dot_nt(q_ref[0, h], kc), cmask)
        oc_ref[h] = _dot(p.astype(BF16), vc)
        p_sum = p_sum + p
    hi, lo = _split_bf16(p_sum)
    imp = _dot(hi, ov_ref[...]) + _dot(lo, ov_ref[...])

    blk = lax.broadcasted_iota(I32, (1, n_blk), 1)
    cur = jnp.right_shift(q_pos, 6)
    visible = blk <= cur
    forced = (blk == 0) | (blk == cur) | (blk == cur - 1)
    score = jnp.where(visible, jnp.where(forced, FORCED, imp), -1.0)
    sel = visible & _topk_mask(score, blk, n_blk)
    mfull = _dot(jnp.where(sel, 1.0, 0.0).astype(BF16), ex_ref[...])
    for c in range(n_tiles):
        msk_ref[c] = mfull[:, c * TQ:(c + 1) * TQ]

    neg_col = jnp.full((TQ, 1), NEG, F32)
    for h in range(NSA_G):
        ms_ref[h] = neg_col
        mw_ref[h] = neg_col
        ls_ref[h] = jnp.zeros((TQ, 1), F32)
        lw_ref[h] = jnp.zeros((TQ, 1), F32)
        as_ref[h] = jnp.zeros((TQ, HEAD_DIM), F32)
        aw_ref[h] = jnp.zeros((TQ, HEAD_DIM), F32)

    def sel_chunk(c, bias_of_head):
        k = ks_ref[0, 0, c]
        v = vs_ref[0, 0, c]
        mk = msk_ref[c] > 0.5
        for h in range(NSA_G):
            _online_step(q_ref[0, h], k, v, jnp.where(mk, bias_of_head(h), NEG), ms_ref, ls_ref, as_ref, h)

    def far_body(c, carry):
        sel_chunk(c, lambda h: far_ref[g * NSA_G + h])
        return carry

    lax.fori_loop(0, jnp.maximum(t - 1, 0), far_body, 0)

    @pl.when(t >= 1)
    def _():
        sel_chunk(t - 1, lambda h: b1_ref[h])

    sel_chunk(t, lambda h: b0_ref[h])

    def win_chunk(c, add_of_head):
        k = kw_ref[0, 0, c]
        v = vw_ref[0, 0, c]
        for h in range(NSA_G):
            _online_step(q_ref[0, h], k, v, add_of_head(h), mw_ref, lw_ref, aw_ref, h)

    win_chunk(t, lambda h: b0_ref[h])

    @pl.when(t >= 1)
    def _():
        win_chunk(t - 1, lambda h: b1_ref[h])

    @pl.when(t >= 2)
    def _():
        inside = lax.broadcasted_iota(I32, (TQ, TQ), 1) > lax.broadcasted_iota(I32, (TQ, TQ), 0)
        win_chunk(t - 2, lambda h: jnp.where(inside, far_ref[g * NSA_G + h], NEG))

    gates = jax.nn.sigmoid(g_ref[0, 0])
    for h in range(NSA_G):
        o_s = as_ref[h] * (1.0 / jnp.maximum(ls_ref[h], TINY))
        o_w = aw_ref[h] * (1.0 / jnp.maximum(lw_ref[h], TINY))
        o = (gates[:, 3 * h:3 * h + 1] * oc_ref[h] + gates[:, 3 * h + 1:3 * h + 2] * o_s
             + gates[:, 3 * h + 2:3 * h + 3] * o_w)
        o_ref[0, h] = o.astype(o_ref.dtype)


def _nsa_prompt_attention(q_hm, kc, vc, ks, vs, kw, vw, gates, b0, b1, far, expand, overlap):
    b, _, l, hd = q_hm.shape
    n_tiles = l // TQ
    ncp = kc.shape[2]
    n_blk = l // SEL_BLOCK
    kv_spec = pl.BlockSpec((1, 1, n_tiles, TQ, hd), lambda bi, g, t, far: (bi, g, 0, 0, 0))
    cmp_spec = pl.BlockSpec((1, 1, ncp, hd), lambda bi, g, t, far: (bi, g, 0, 0))
    bias_spec = pl.BlockSpec((NSA_G, TQ, TQ), lambda bi, g, t, far: (g, 0, 0))
    kern = functools.partial(_nsa_prompt_kernel, n_tiles=n_tiles, n_cmp=ncp - 1, n_blk=n_blk)
    col = lambda: pltpu.VMEM((NSA_G, TQ, 1), F32)
    acc = lambda: pltpu.VMEM((NSA_G, TQ, hd), F32)
    return pl.pallas_call(
        kern,
        grid_spec=pltpu.PrefetchScalarGridSpec(
            num_scalar_prefetch=1,
            grid=(b, NSA_KV, n_tiles),
            in_specs=[pl.BlockSpec((1, NSA_G, TQ, hd), lambda bi, g, t, far: (bi, g, t, 0)),
                      cmp_spec, cmp_spec, kv_spec, kv_spec, kv_spec, kv_spec,
                      pl.BlockSpec((1, 1, TQ, 3 * NSA_G), lambda bi, g, t, far: (bi, g, t, 0)),
                      bias_spec, bias_spec,
                      pl.BlockSpec((n_blk, l), lambda bi, g, t, far: (0, 0)),
                      pl.BlockSpec((ncp, n_blk), lambda bi, g, t, far: (0, 0))],
            out_specs=pl.BlockSpec((1, NSA_G, TQ, hd), lambda bi, g, t, far: (bi, g, t, 0)),
            scratch_shapes=[pltpu.VMEM((n_tiles, TQ, TQ), F32), acc(),
                            col(), col(), acc(), col(), col(), acc()]),
        out_shape=jax.ShapeDtypeStruct(q_hm.shape, BF16),
        compiler_params=_cparams(("parallel", "parallel", "arbitrary")),
        name="nsa_prompt_attention",
    )(far, q_hm, kc, vc, ks, vs, kw, vw, gates, b0, b1, expand, overlap)


def _swa_prompt_kernel(sink_ref, q_ref, k_ref, v_ref, b0_ref, b1_ref, o_ref, m_ref, l_ref, a_ref):
    g = pl.program_id(1)
    t = pl.program_id(2)
    for h in range(SWA_G):
        m_ref[h] = jnp.full((TQ, 1), sink_ref[g * SWA_G + h], F32)
        l_ref[h] = jnp.ones((TQ, 1), F32)
        a_ref[h] = jnp.zeros((TQ, HEAD_DIM), F32)

    def chunk(c, bias_ref):
        k = k_ref[0, 0, c]
        v = v_ref[0, 0, c]
        for h in range(SWA_G):
            _online_step(q_ref[0, h], k, v, bias_ref[h], m_ref, l_ref, a_ref, h)

    chunk(t, b0_ref)

    @pl.when(t >= 1)
    def _():
        chunk(t - 1, b1_ref)

    for h in range(SWA_G):
        o_ref[0, h] = (a_ref[h] * (1.0 / l_ref[h])).astype(o_ref.dtype)


def _swa_prompt_attention(q_hm, k, v, b0, b1, sinks):
    b, _, l, hd = q_hm.shape
    n_tiles = l // TQ
    kv_spec = pl.BlockSpec((1, 1, n_tiles, TQ, hd), lambda bi, g, t, s: (bi, g, 0, 0, 0))
    bias_spec = pl.BlockSpec((SWA_G, TQ, TQ), lambda bi, g, t, s: (g, 0, 0))
    return pl.pallas_call(
        _swa_prompt_kernel,
        grid_spec=pltpu.PrefetchScalarGridSpec(
            num_scalar_prefetch=1,
            grid=(b, SWA_KV, n_tiles),
            in_specs=[pl.BlockSpec((1, SWA_G, TQ, hd), lambda bi, g, t, s: (bi, g, t, 0)),
                      kv_spec, kv_spec, bias_spec, bias_spec],
            out_specs=pl.BlockSpec((1, SWA_G, TQ, hd), lambda bi, g, t, s: (bi, g, t, 0)),
            scratch_shapes=[pltpu.VMEM((SWA_G, TQ, 1), F32), pltpu.VMEM((SWA_G, TQ, 1), F32),
                            pltpu.VMEM((SWA_G, TQ, hd), F32)]),
        out_shape=jax.ShapeDtypeStruct(q_hm.shape, BF16),
        compiler_params=_cparams(("parallel", "parallel", "arbitrary")),
        name="swa_prompt_attention",
    )(sinks, q_hm, k, v, b0, b1)


def _compress_kernel(pt_ref, *refs, npg, n_steps, feat_major):
    page_refs = refs[:npg]
    wbd_ref, b1_ref, w2_ref, kc_ref, vc_ref, x_s, tp_s = refs[npg:]
    s = pl.program_id(1)
    cpp = PAGE // CMP_STRIDE
    n_q = x_s.shape[1]
    for ip in range(npg // 2):
        base = pl.multiple_of((s * npg + 2 * ip) * cpp, 2 * cpp)
        if feat_major:
            for pi in range(2):
                for c in range(n_q):
                    tp_s[pi, c] = page_refs[2 * ip + pi][c * LANE:(c + 1) * LANE, :].T
        for j in range(CMP_STRIDE):
            for c in range(n_q):
                if feat_major:
                    halves = [tp_s[pi, c, pl.ds(j, cpp, stride=CMP_STRIDE), :] for pi in range(2)]
                else:
                    halves = [page_refs[2 * ip + pi][pl.ds(j * n_q + c, cpp, stride=CMP_STRIDE * n_q), :]
                              for pi in range(2)]
                x_s[j, c, pl.ds(base, 2 * cpp), :] = jnp.concatenate(halves, axis=0).astype(BF16)

    @pl.when(s == n_steps - 1)
    def _():
        n_chunks = x_s.shape[2]
        rows = lax.broadcasted_iota(I32, (n_chunks, 1), 0)
        for kv, out_ref in ((0, kc_ref), (1, vc_ref)):
            for gp in range(NSA_KV // 2):
                c = kv * (NSA_KV // 2) + gp
                y = jnp.zeros((n_chunks, 2 * LANE), F32)
                for jj in range(CMP_STRIDE // 2):
                    lhs = jnp.concatenate([x_s[2 * jj, c], x_s[2 * jj + 1, c]], axis=1)
                    y = y + _dot(lhs, wbd_ref[kv, jj])
                pre = y[:, :LANE] + pltpu.roll(y[:, LANE:], n_chunks - 1, 0) + b1_ref[kv]
                hid = jax.nn.gelu(pre)
                out = _dot(hid.astype(BF16), w2_ref[kv])
                out = jnp.where(rows < n_chunks - 1, out, 0.0)
                out_ref[0, :, gp * LANE:(gp + 1) * LANE] = out.astype(out_ref.dtype)


def _compress(pages, a, page_table, wbd, b1t, w2bd, feat_major):
    nseq, n_pages = page_table.shape
    npg = min(NPG_MAX, n_pages)
    n_steps = n_pages // npg
    n_chunks = n_pages * PAGE // CMP_STRIDE
    blk = pages.shape[2:]
    n_q = 2 * NSA_KV * HEAD_DIM // LANE

    def page_spec(i):
        return pl.BlockSpec((None, None) + blk, lambda b, s, pt: (a, pt[b, s * npg + i], 0, 0))

    out_spec = pl.BlockSpec((1, n_chunks, NSA_KV * HEAD_DIM), lambda b, s, pt: (b, 0, 0))
    kern = functools.partial(_compress_kernel, npg=npg, n_steps=n_steps, feat_major=feat_major)
    fix3 = lambda b, s, pt: (0, 0, 0)
    return pl.pallas_call(
        kern,
        grid_spec=pltpu.PrefetchScalarGridSpec(
            num_scalar_prefetch=1,
            grid=(nseq, n_steps),
            in_specs=[page_spec(i) for i in range(npg)] + [
                pl.BlockSpec(wbd.shape, lambda b, s, pt: (0, 0, 0, 0)),
                pl.BlockSpec(b1t.shape, fix3), pl.BlockSpec(w2bd.shape, fix3)],
            out_specs=[out_spec, out_spec],
            scratch_shapes=[pltpu.VMEM((CMP_STRIDE, n_q, n_chunks, LANE), BF16),
                            pltpu.VMEM((2, n_q, PAGE, LANE), F32)]),
        out_shape=[jax.ShapeDtypeStruct((nseq, n_chunks, NSA_KV * HEAD_DIM), BF16)] * 2,
        compiler_params=_cparams(("parallel", "arbitrary")),
        name="compress",
    )(page_table, *([pages] * npg), wbd, b1t, w2bd)


def _diag_blocks(o, n_groups, rows_per_group):
    return jnp.concatenate(
        [o[g * rows_per_group:(g + 1) * rows_per_group, g * HEAD_DIM:(g + 1) * HEAD_DIM] for g in range(n_groups)],
        axis=0)


def _nsa_sample_kernel(pt_ref, *refs, npg, n_steps, n_cmp, n_blk):
    page_refs = refs[:npg]
    (q_ref, kc_ref, vc_ref, tails_ref, tailw_ref, win_ref, g_ref, farc_ref, blast_ref, btail_ref, bwin_ref,
     cpos_ref, hsum_ref, ov_ref, o_ref, msk_ref, tsel_ref, oc_ref, m_ref, l_ref, a_ref) = refs[npg:]
    s = pl.program_id(1)
    q = q_ref[0]
    n_rows = q.shape[0]
    gw = NSA_KV * HEAD_DIM
    bps = 2 * npg

    @pl.when(s == 0)
    def _():
        kc = kc_ref[0]
        ncp = kc.shape[0]
        n_idx = lax.broadcasted_iota(I32, (1, ncp), 1)
        cmask = (CMP_STRIDE * n_idx + (CMP_LEN - 1) <= cpos_ref[...]) & (n_idx < n_cmp)
        p = _masked_softmax(_dot_nt(q, kc), cmask)
        oc_ref[...] = _dot(p.astype(BF16), vc_ref[0])
        hi, lo = _split_bf16(p)
        p_grp = _dot(hsum_ref[...], hi) + _dot(hsum_ref[...], lo)
        hi, lo = _split_bf16(p_grp)
        imp = _dot(hi, ov_ref[...]) + _dot(lo, ov_ref[...])
        nbp = imp.shape[1]
        blk = lax.broadcasted_iota(I32, (1, nbp), 1)
        cur = n_blk - 1
        visible = blk <= cur
        forced = (blk == 0) | (blk == cur) | (blk == cur - 1)
        score = jnp.where(visible, jnp.where(forced, FORCED, imp), -1.0)
        sel = jnp.where(visible & _topk_mask(score, blk, n_blk), 1.0, 0.0)
        for st in range(n_steps):
            msk_ref[st, :, 0:bps] = sel[:, st * bps:(st + 1) * bps]
        tsel_ref[...] = sel[:, n_blk - 1:n_blk]
        m_ref[...] = jnp.full((n_rows, 1), NEG, F32)
        l_ref[...] = jnp.zeros((n_rows, 1), F32)
        a_ref[...] = jnp.zeros((n_rows, gw), F32)

    ms = msk_ref[s]
    lane = lax.broadcasted_iota(I32, (1, PAGE), 1)
    farc = farc_ref[...]
    for i in range(npg):
        page = page_refs[i]
        k = page[0:gw, :].astype(BF16)
        v = page[gw:2 * gw, :].astype(BF16)
        mk = jnp.where(lane < SEL_BLOCK, ms[:, 2 * i:2 * i + 1], ms[:, 2 * i + 1:2 * i + 2]) > 0.5
        if i == npg - 1:
            bias = jnp.where(s == n_steps - 1, blast_ref[...], farc)
        else:
            bias = farc
        _online_step(q, k, v, jnp.where(mk, bias, NEG), m_ref, l_ref, a_ref, slice(None), feat_major=True)

    @pl.when(s == n_steps - 1)
    def _():
        tails = tails_ref[0]
        _online_step(q, tails[:, 0:gw], tails[:, gw:2 * gw], jnp.where(tsel_ref[...] > 0.5, btail_ref[...], NEG),
                     m_ref, l_ref, a_ref, slice(None))
        o_s = a_ref[...] * (1.0 / jnp.maximum(l_ref[...], TINY))

        tailw = tailw_ref[0]
        state = (jnp.full((n_rows, 1), NEG, F32), jnp.zeros((n_rows, 1), F32), jnp.zeros((n_rows, gw), F32))
        state = _online_value(q, tailw[:, 0:gw], tailw[:, gw:2 * gw], btail_ref[...], state)
        wl = win_ref.shape[3]
        for c in range(wl // PAGE):
            rows = win_ref[0, 0, :, c * PAGE:(c + 1) * PAGE]
            state = _online_value(q, rows[0:gw, :].astype(BF16), rows[gw:2 * gw, :].astype(BF16),
                                  bwin_ref[:, c * PAGE:(c + 1) * PAGE], state, feat_major=True)
        o_w = state[2] * (1.0 / jnp.maximum(state[1], TINY))

        gates = jax.nn.sigmoid(g_ref[0])
        o = gates[:, 0:1] * oc_ref[...] + gates[:, 1:2] * o_s + gates[:, 2:3] * o_w
        o_ref[0] = _diag_blocks(o, NSA_KV, n_rows // NSA_KV)


def _nsa_sample_attention(q_bd, kc, vc, cache4, a, page_table, tail_s, tail_w, win4, gates, farc, blast, btail, bwin,
                          cpos, hsum, overlap, n_blk):
    bs, n_rows, gw = q_bd.shape
    n_pages = page_table.shape[1]
    npg = min(NPG_MAX, n_pages)
    n_steps = n_pages // npg
    ncp = kc.shape[1]
    wl = win4.shape[3]

    def page_spec(i):
        return pl.BlockSpec((None, None, 2 * gw, PAGE), lambda b, s, pt: (a, pt[b, s * npg + i], 0, 0))

    per_b3 = lambda b, s, pt: (b, 0, 0)
    fix2 = lambda b, s, pt: (0, 0)
    full2 = lambda arr: pl.BlockSpec(arr.shape, fix2)
    kern = functools.partial(_nsa_sample_kernel, npg=npg, n_steps=n_steps, n_cmp=ncp - 1, n_blk=n_blk)
    return pl.pallas_call(
        kern,
        grid_spec=pltpu.PrefetchScalarGridSpec(
            num_scalar_prefetch=1,
            grid=(bs, n_steps),
            in_specs=[page_spec(i) for i in range(npg)] + [
                pl.BlockSpec((1, n_rows, gw), per_b3),
                pl.BlockSpec((1, ncp, gw), per_b3), pl.BlockSpec((1, ncp, gw), per_b3),
                pl.BlockSpec((1, PAGE, 2 * gw), per_b3), pl.BlockSpec((1, PAGE, 2 * gw), per_b3),
                pl.BlockSpec((1, 1, 2 * gw, wl), lambda b, s, pt: (a, b, 0, 0)),
                pl.BlockSpec((1, n_rows, 3), per_b3),
                full2(farc), full2(blast), full2(btail), full2(bwin), full2(cpos), full2(hsum), full2(overlap)],
            out_specs=pl.BlockSpec((1, n_rows, HEAD_DIM), per_b3),
            scratch_shapes=[pltpu.VMEM((n_steps, n_rows, LANE), F32), pltpu.VMEM((n_rows, 1), F32),
                            pltpu.VMEM((n_rows, gw), F32), pltpu.VMEM((n_rows, 1), F32),
                            pltpu.VMEM((n_rows, 1), F32), pltpu.VMEM((n_rows, gw), F32)]),
        out_shape=jax.ShapeDtypeStruct((bs, n_rows, HEAD_DIM), F32),
        compiler_params=_cparams(("parallel", "arbitrary")),
        name="nsa_sample_attention",
    )(page_table, *([cache4] * npg), q_bd, kc, vc, tail_s, tail_w, win4, gates, farc, blast, btail, bwin,
      cpos, hsum, overlap)


def _swa_sample_kernel(q_ref, buf_ref, tail_ref, sink_ref, bbuf_ref, btail_ref, o_ref):
    q = q_ref[0]
    n_rows = q.shape[0]
    gw = SWA_KV * HEAD_DIM
    state = (sink_ref[...], jnp.ones((n_rows, 1), F32), jnp.zeros((n_rows, gw), F32))
    tail = tail_ref[0]
    state = _online_value(q, tail[:, 0:gw], tail[:, gw:2 * gw], btail_ref[...], state)
    wl = buf_ref.shape[3]
    for c in range(wl // PAGE):
        rows = buf_ref[0, 0, :, c * PAGE:(c + 1) * PAGE]
        state = _online_value(q, rows[0:gw, :].astype(BF16), rows[gw:2 * gw, :].astype(BF16),
                              bbuf_ref[:, c * PAGE:(c + 1) * PAGE], state, feat_major=True)
    o = state[2] * (1.0 / state[1])
    o_ref[0] = _diag_blocks(o, SWA_KV, n_rows // SWA_KV)


def _swa_sample_attention(q_bd, buf4, a, tail, sinkc, bbuf, btail):
    bs, n_rows, gw = q_bd.shape
    wl = buf4.shape[3]
    per_b3 = lambda b: (b, 0, 0)
    full2 = lambda arr: pl.BlockSpec(arr.shape, lambda b: (0, 0))
    return pl.pallas_call(
        _swa_sample_kernel,
        grid=(bs,),
        in_specs=[pl.BlockSpec((1, n_rows, gw), per_b3),
                  pl.BlockSpec((1, 1, 2 * gw, wl), lambda b: (a, b, 0, 0)),
                  pl.BlockSpec((1, PAGE, 2 * gw), per_b3),
                  full2(sinkc), full2(bbuf), full2(btail)],
        out_specs=pl.BlockSpec((1, n_rows, HEAD_DIM), per_b3),
        out_shape=jax.ShapeDtypeStruct((bs, n_rows, HEAD_DIM), F32),
        compiler_params=_cparams(("parallel",)),
        name="swa_sample_attention",
    )(q_bd, buf4, tail, sinkc, bbuf, btail)


def _rel_bucket(dist):
    n = jnp.maximum(dist, 0)
    nf = jnp.maximum(n, 1).astype(F32)
    far = REL_MAX_EXACT + (jnp.log(nf / REL_MAX_EXACT) / math.log(REL_MAX_DIST / REL_MAX_EXACT)
                           * (REL_BUCKETS - REL_MAX_EXACT)).astype(I32)
    return jnp.where(n < REL_MAX_EXACT, n, jnp.minimum(far, REL_BUCKETS - 1))


def _bias_of(rel_bias, dist):
    return jnp.moveaxis(rel_bias[_rel_bucket(dist)], -1, 0)


def _prompt_bias_tiles(rel_bias, window):
    qi = jnp.arange(TQ)[:, None]
    kj = jnp.arange(TQ)[None, :]
    d0 = qi - kj
    d1 = TQ + qi - kj
    b0 = jnp.where((d0 >= 0) & (d0 < window), _bias_of(rel_bias, d0), NEG)
    b1 = jnp.where(d1 < window, _bias_of(rel_bias, d1), NEG)
    return b0.astype(F32), b1.astype(F32)


def _sample_bias_tiles(rel_bias, t, wl, window):
    n_rows = N_HEADS * t
    head = jnp.arange(n_rows) // t
    q = (jnp.arange(n_rows) % t)[:, None]
    kj = jnp.arange(PAGE)[None, :]
    d_tail = q - kj
    tail_all = _bias_of(rel_bias, d_tail)[head, jnp.arange(n_rows)]
    btail = jnp.where((d_tail >= 0) & (kj < t), tail_all, NEG)
    i = jnp.arange(wl)[None, :]
    d_buf = wl + q - i
    buf_all = _bias_of(rel_bias, d_buf)[head, jnp.arange(n_rows)]
    bbuf = jnp.where(d_buf < window, buf_all, NEG)
    return btail.astype(F32), bbuf.astype(F32), head, q


def _block_diag_q(q, n_kv, t):
    bs = q.shape[0]
    grp = N_HEADS // n_kv
    q5 = q.reshape(bs, t, n_kv, grp, HEAD_DIM).transpose(0, 2, 3, 1, 4)
    eye = jnp.eye(n_kv, dtype=q.dtype)
    qbd = jnp.einsum('bghqd,gk->bghqkd', q5, eye)
    return qbd.reshape(bs, N_HEADS * t, n_kv * HEAD_DIM).astype(BF16)


def _rows_to_tokens(o, t):
    bs = o.shape[0]
    return o.reshape(bs, N_HEADS, t, HEAD_DIM).transpose(0, 2, 1, 3).reshape(bs * t, N_HEADS * HEAD_DIM)


def _head_major(x, n_heads):
    b, l, _ = x.shape
    return x.reshape(b, l, n_heads, HEAD_DIM).transpose(0, 2, 1, 3).astype(BF16)


def _pad_tail(rows, t):
    return jnp.pad(rows, ((0, 0), (0, PAGE - t), (0, 0))).astype(BF16)


def _compress_weights(w1, b1, w2):
    w = w1.reshape(2, 2, CMP_STRIDE // 2, 2, HEAD_DIM, HEAD_DIM)
    eye = jnp.eye(2, dtype=w1.dtype)
    wbd = jnp.einsum('kpjlde,gh->kjlgdphe', w, eye).reshape(2, CMP_STRIDE // 2, 4 * HEAD_DIM, 4 * HEAD_DIM)
    b1t = jnp.tile(b1, (1, 2)).reshape(2, 1, 2 * HEAD_DIM)
    w2bd = jnp.einsum('kde,gh->kgdhe', w2, eye).reshape(2, 2 * HEAD_DIM, 2 * HEAD_DIM)
    return wbd.astype(BF16), b1t.astype(F32), w2bd.astype(BF16)


def _overlap(n_cmp_rows, n_blk_cols):
    n = CMP_STRIDE * jnp.arange(n_cmp_rows)[:, None]
    j = jnp.arange(n_blk_cols)[None, :]
    return ((n < SEL_BLOCK * (j + 1)) & (n + CMP_LEN > SEL_BLOCK * j)).astype(BF16)


def _route(logits):
    n_tok = logits.shape[0]
    g_prob = jax.nn.softmax(logits[:, :N_GROUPS], -1)
    g_p, g_i = lax.top_k(g_prob, 1)
    e_logits = logits[:, N_GROUPS:N_GROUPS + N_EXPERTS].reshape(n_tok, N_GROUPS, EPG)
    e_logits = jnp.take_along_axis(e_logits, g_i[:, :, None], axis=1)[:, 0]
    e_p, e_i = lax.top_k(jax.nn.softmax(e_logits, -1), TOP_K)
    w = e_p / jnp.sum(e_p, -1, keepdims=True) * g_p
    ids = (g_i * EPG + e_i).reshape(-1)
    return ids, w


def _dispatch_plan(ids):
    n = ids.shape[0]
    order = jnp.argsort(ids)
    sid = ids[order]
    counts = jnp.bincount(ids, length=N_EXPERTS)
    padded = (counts + TM - 1) // TM * TM
    pad_end = jnp.cumsum(padded)
    pad_start = pad_end - padded
    raw_start = jnp.cumsum(counts) - counts
    dest = (pad_start[sid] + jnp.arange(n) - raw_start[sid]).astype(I32)
    n_blocks = -(-n // TM) + N_EXPERTS
    src = jnp.zeros((n_blocks * TM,), I32).at[dest].set((order // TOP_K).astype(I32))
    blk_start = jnp.arange(n_blocks) * TM
    blk_expert = jnp.minimum(jnp.sum(pad_end[None, :] <= blk_start[:, None], axis=1), N_EXPERTS - 1).astype(I32)
    n_used = (pad_end[-1:] // TM).astype(I32)
    pos = jnp.zeros((n,), I32).at[order].set(dest)
    return src, blk_expert, n_used, pos


def _moe(hn, hb, logits, w_gate, w_up, w_down, g, b):
    ids, w = _route(logits)
    src, blk_expert, n_used, pos = _dispatch_plan(ids)
    out = _experts(hb[src], blk_expert, n_used, w_gate, w_up, w_down)
    return _combine_ln(hn, out[pos[0::2]], out[pos[1::2]], w.astype(F32), g, b)


def kernel(x_prompt, x_sample, cache_nsa_cmp, cache_nsa_sel, state_nsa_win, state_swa_win, page_table, rel_bias, nsa_w_in, nsa_b_in, nsa_cmp_w1, nsa_cmp_b1, nsa_cmp_w2, nsa_w_out, swa_w_in, swa_sinks, swa_w_out, moe_w_group, moe_b_group, moe_w_expert, moe_b_expert, moe_w_gate, moe_w_up, moe_w_down, ln_g, ln_b):
    bp, l, d = x_prompt.shape
    bs, t, _ = x_sample.shape
    n_pages = page_table.shape[1]
    past = n_pages * PAGE
    ntp = bp * l
    hq = N_HEADS * HEAD_DIM
    gw_n = NSA_KV * HEAD_DIM
    gw_s = SWA_KV * HEAD_DIM
    assert l % TQ == 0 and NSA_WINDOW == 2 * TQ and SWA_WINDOW <= TQ and TQ >= REL_MAX_DIST
    assert ntp % TM == 0 and t <= CMP_LEN - 1 and t <= SEL_BLOCK and n_pages % 2 == 0
    nsa_wl = state_nsa_win.shape[2]
    swa_wl = state_swa_win.shape[2]
    assert nsa_wl % PAGE == 0 and swa_wl % PAGE == 0
    nts = bs * t
    row_pad = -(ntp + nts) % TM

    x = jnp.concatenate([x_prompt.reshape(ntp, d), x_sample.reshape(nts, d), jnp.zeros((row_pad, d), F32)], axis=0)

    far = rel_bias[REL_BUCKETS - 1].astype(F32)
    nb0, nb1 = _prompt_bias_tiles(rel_bias, NSA_WINDOW)
    sb0, sb1 = _prompt_bias_tiles(rel_bias, SWA_WINDOW)
    n_blk_p = l // SEL_BLOCK
    expand = (jnp.arange(l)[None, :] // SEL_BLOCK == jnp.arange(n_blk_p)[:, None]).astype(BF16)
    ov_p = _overlap(l // CMP_STRIDE, n_blk_p)
    n_blk_s = past // SEL_BLOCK + 1
    nbp_s = -(-n_blk_s // LANE) * LANE
    ov_s = _overlap(past // CMP_STRIDE, nbp_s)
    n_rows = N_HEADS * t
    btail, nbwin, head, qrow = _sample_bias_tiles(rel_bias, t, nsa_wl, NSA_WINDOW)
    _, sbbuf, _, _ = _sample_bias_tiles(rel_bias, t, swa_wl, SWA_WINDOW)
    farc = far[head][:, None]
    d_last = PAGE + qrow - jnp.arange(PAGE)[None, :]
    blast = _bias_of(rel_bias, d_last)[head, jnp.arange(n_rows)].astype(F32)
    cpos = (past + qrow).astype(I32)
    grp_rows = NSA_G * t
    r = jnp.arange(n_rows)
    hsum = ((r[:, None] // grp_rows == r[None, :] // grp_rows) & (r[:, None] % t == r[None, :] % t)).astype(BF16)
    prompt_pages = jnp.arange(ntp // PAGE, dtype=I32).reshape(bp, l // PAGE)

    def feat_major(z):
        return jnp.transpose(z, (0, 1, 3, 4, 5, 2)).reshape(z.shape[0], z.shape[1], -1, z.shape[2])

    cache_cmp4 = feat_major(cache_nsa_cmp)
    cache_sel4 = feat_major(cache_nsa_sel)
    nsa_win4 = feat_major(state_nsa_win)
    swa_win4 = feat_major(state_swa_win)

    outs = {k: [] for k in ('cmp_p', 'cmp_s', 'sel_p', 'sel_s', 'nwin_p', 'nwin_s', 'swin_p', 'swin_s')}
    for layer in range(DEPTH):
        j = layer // 2
        if layer % 2 == 0:
            nsa_in = nsa_w_in.shape[-1]
            n_pad = -(-nsa_in // LANE) * LANE
            w_in = jnp.pad(nsa_w_in[j], ((0, 0), (0, n_pad - nsa_in))).astype(BF16)
            b_in = jnp.pad(nsa_b_in[j], (0, n_pad - nsa_in)).reshape(1, n_pad)
            h = _project(x, w_in, b_in)
            hp = h[:ntp].reshape(bp, l, n_pad)
            hs = h[ntp:ntp + nts].reshape(bs, t, n_pad)
            c0, c1, c2, c3 = hq, hq + 2 * gw_n, hq + 4 * gw_n, hq + 6 * gw_n
            wbd, b1t, w2bd = _compress_weights(nsa_cmp_w1[j], nsa_cmp_b1[j], nsa_cmp_w2[j])

            kvc, kvs, kvw = hp[..., c0:c1], hp[..., c1:c2], hp[..., c2:c3]
            kc, vc = _compress(kvc.reshape(1, ntp // PAGE, PAGE * 2 * gw_n // LANE, LANE), 0, prompt_pages,
                               wbd, b1t, w2bd, feat_major=False)
            chunked = lambda z: _head_major(z, NSA_KV).reshape(bp, NSA_KV, l // TQ, TQ, HEAD_DIM)
            gates_p = hp[..., c3:c3 + 3 * N_HEADS].reshape(bp, l, NSA_KV, 3 * NSA_G).transpose(0, 2, 1, 3)
            o_p = _nsa_prompt_attention(
                _head_major(hp[..., :hq] * SCALE, N_HEADS), _head_major(kc, NSA_KV), _head_major(vc, NSA_KV),
                chunked(kvs[..., :gw_n]), chunked(kvs[..., gw_n:]), chunked(kvw[..., :gw_n]), chunked(kvw[..., gw_n:]),
                gates_p, nb0, nb1, far, expand, ov_p)
            o_p = o_p.transpose(0, 2, 1, 3).reshape(ntp, hq)
            kv_shape = (2, NSA_KV, HEAD_DIM)
            outs['cmp_p'].append(kvc.reshape(bp, l, *kv_shape))
            outs['sel_p'].append(kvs.reshape(bp, l, *kv_shape))
            wl_p = min(NSA_WINDOW, l)
            outs['nwin_p'].append(kvw[:, l - wl_p:].reshape(bp, wl_p, *kv_shape))

            kcs, vcs = _compress(cache_cmp4, j, page_table, wbd, b1t, w2bd, feat_major=True)
            gates_s = hs[..., c3:c3 + 3 * N_HEADS].reshape(bs, t, N_HEADS, 3).transpose(0, 2, 1, 3)
            o_s = _nsa_sample_attention(
                _block_diag_q(hs[..., :hq] * SCALE, NSA_KV, t), kcs, vcs, cache_sel4, j, page_table,
                _pad_tail(hs[..., c1:c2], t), _pad_tail(hs[..., c2:c3], t), nsa_win4,
                gates_s.reshape(bs, n_rows, 3), farc, blast, btail, nbwin, cpos, hsum, ov_s, n_blk_s)
            outs['cmp_s'].append(hs[..., c0:c1].reshape(bs, t, *kv_shape))
            outs['sel_s'].append(hs[..., c1:c2].reshape(bs, t, *kv_shape))
            outs['nwin_s'].append(jnp.concatenate(
                [state_nsa_win[j][:, t:], hs[..., c2:c3].reshape(bs, t, *kv_shape)], axis=1))
            w_out = nsa_w_out[j]
        else:
            swa_in = swa_w_in.shape[-1]
            h = _project(x, swa_w_in[j].astype(BF16), jnp.zeros((1, swa_in), F32))
            hp = h[:ntp].reshape(bp, l, swa_in)
            hs = h[ntp:ntp + nts].reshape(bs, t, swa_in)
            kp, vp = hp[..., hq:hq + gw_s], hp[..., hq + gw_s:]
            chunked = lambda z: _head_major(z, SWA_KV).reshape(bp, SWA_KV, l // TQ, TQ, HEAD_DIM)
            o_p = _swa_prompt_attention(_head_major(hp[..., :hq] * SCALE, N_HEADS), chunked(kp), chunked(vp),
                                        sb0, sb1, swa_sinks[j].astype(F32))
            o_p = o_p.transpose(0, 2, 1, 3).reshape(ntp, hq)
            wl_p = min(SWA_WINDOW, l)
            sw_shape = (SWA_KV, HEAD_DIM)
            outs['swin_p'].append(jnp.stack([kp[:, l - wl_p:].reshape(bp, wl_p, *sw_shape),
                                             vp[:, l - wl_p:].reshape(bp, wl_p, *sw_shape)], axis=2))
            o_s = _swa_sample_attention(_block_diag_q(hs[..., :hq] * SCALE, SWA_KV, t), swa_win4, j,
                                        _pad_tail(hs[..., hq:], t), swa_sinks[j][head][:, None].astype(F32),
                                        sbbuf, btail)
            new_rows = jnp.stack([hs[..., hq:hq + gw_s].reshape(bs, t, *sw_shape),
                                  hs[..., hq + gw_s:].reshape(bs, t, *sw_shape)], axis=2)
            outs['swin_s'].append(jnp.concatenate([state_swa_win[j][:, t:], new_rows], axis=1))
            w_out = swa_w_out[j]

        o = jnp.concatenate([o_p, _rows_to_tokens(o_s, t).astype(BF16), jnp.zeros((row_pad, hq), BF16)], axis=0)
        wr = jnp.pad(jnp.concatenate([moe_w_group[layer], moe_w_expert[layer]], axis=1),
                     ((0, 0), (0, ROUTER_PAD - N_GROUPS - N_EXPERTS)))
        br = jnp.pad(jnp.concatenate([moe_b_group[layer], moe_b_expert[layer]]),
                     (0, ROUTER_PAD - N_GROUPS - N_EXPERTS)).reshape(1, ROUTER_PAD)
        wr_hi, wr_lo = _split_bf16(wr)
        hn, hb, logits = _outproj_ln_router(o, w_out.astype(BF16), x, ln_g[layer, 0].reshape(1, d),
                                            ln_b[layer, 0].reshape(1, d), wr_hi, wr_lo, br)
        x = _moe(hn, hb, logits, moe_w_gate[layer], moe_w_up[layer], moe_w_down[layer],
                 ln_g[layer, 1].reshape(1, d), ln_b[layer, 1].reshape(1, d))

    st = lambda k: jnp.stack(outs[k])
    return (x[:ntp].reshape(bp, l, d), x[ntp:ntp + nts].reshape(bs, t, d), st('cmp_p'), st('cmp_s'), st('sel_p'), st('sel_s'),
            st('nwin_p'), st('nwin_s'), st('swin_p'), st('swin_s'))
```

```python
import functools
import math

import jax
import jax.numpy as jnp
from jax import lax
from jax.experimental import pallas as pl
from jax.experimental.pallas import tpu as pltpu

F32 = jnp.float32
BF16 = jnp.bfloat16
I32 = jnp.int32

D_MODEL = 1024
N_HEADS = 16
HEAD_DIM = 64
NSA_KV = 4
NSA_G = N_HEADS // NSA_KV
CMP_STRIDE = 16
CMP_LEN = 32
SEL_BLOCK = 64
N_SEL = 16
NSA_WINDOW = 512
SWA_KV = 2
SWA_G = N_HEADS // SWA_KV
SWA_WINDOW = 128
REL_BUCKETS = 32
REL_MAX_EXACT = 16
REL_MAX_DIST = 128
N_GROUPS = 4
EPG = 8
N_EXPERTS = N_GROUPS * EPG
TOP_K = 2
D_EXPERT = 512
PAGE = 128
DEPTH = 4
DN_ALPHA = (2 * DEPTH) ** 0.25
LN_EPS = 1e-5
SCALE = HEAD_DIM ** -0.5
NEG = -1e30
NEG_FLOOR = -1e29
TINY = 1e-30
FORCED = 1e4

LANE = 128
TQ = 256
TM = 256
ROUTER_PAD = LANE
NPG_MAX = 16
PAGES_PER_STEP = 4
VMEM_LIMIT = 52 * 1024 * 1024


def _cparams(sem):
    return pltpu.CompilerParams(dimension_semantics=sem, vmem_limit_bytes=VMEM_LIMIT)


def _dot_nt(a, b):
    return lax.dot_general(a, b, (((1,), (1,)), ((), ())), preferred_element_type=F32)


def _dot(a, b):
    return jnp.dot(a, b, preferred_element_type=F32)


def _split_bf16(x):
    hi = x.astype(BF16)
    lo = (x - hi.astype(F32)).astype(BF16)
    return hi, lo


def _proj_kernel(x_ref, w_ref, b_ref, o_ref):
    o_ref[...] = _dot(x_ref[...].astype(BF16), w_ref[...]) + b_ref[...]


def _project(x, w_bf16, bias):
    m, k = x.shape
    n = w_bf16.shape[1]
    return pl.pallas_call(
        _proj_kernel,
        grid=(m // TM,),
        in_specs=[pl.BlockSpec((TM, k), lambda i: (i, 0)),
                  pl.BlockSpec((k, n), lambda i: (0, 0)),
                  pl.BlockSpec((1, n), lambda i: (0, 0))],
        out_specs=pl.BlockSpec((TM, n), lambda i: (i, 0)),
        out_shape=jax.ShapeDtypeStruct((m, n), F32),
        compiler_params=_cparams(("parallel",)),
        name="project",
    )(x, w_bf16, bias)


def _layer_norm(z, g, b):
    mu = jnp.mean(z, -1, keepdims=True)
    zc = z - mu
    var = jnp.mean(zc * zc, -1, keepdims=True)
    return zc * lax.rsqrt(var + LN_EPS) * g + b


def _first_lane_of_max(cand, lanef):
    top = jnp.max(cand, -1, keepdims=True)
    return top, jnp.min(jnp.where(cand == top, lanef, float(cand.shape[-1])), -1, keepdims=True)


def _route_rows(logits):
    lanef = lax.broadcasted_iota(I32, (1, ROUTER_PAD), 1).astype(F32)
    is_g = lanef < float(N_GROUPS)
    lg = jnp.where(is_g, logits, NEG)
    eg = jnp.where(is_g, jnp.exp(lg - jnp.max(lg, -1, keepdims=True)), 0.0)
    g_prob = eg / jnp.sum(eg, -1, keepdims=True)
    g_p, g_i = _first_lane_of_max(jnp.where(is_g, g_prob, -1.0), lanef)
    lo = float(N_GROUPS) + float(EPG) * g_i
    in_grp = (lanef >= lo) & (lanef < lo + float(EPG))
    le = jnp.where(in_grp, logits, NEG)
    ee = jnp.where(in_grp, jnp.exp(le - jnp.max(le, -1, keepdims=True)), 0.0)
    cand = jnp.where(in_grp, ee / jnp.sum(ee, -1, keepdims=True), -1.0)
    p1, i1 = _first_lane_of_max(cand, lanef)
    p2, i2 = _first_lane_of_max(jnp.where(lanef == i1, -1.0, cand), lanef)
    den = p1 + p2
    first = lanef == 0.0
    ids = jnp.where(first, i1, i2) - float(N_GROUPS)
    w = jnp.where(first, p1 / den * g_p, p2 / den * g_p)
    return ids[:, 0:TOP_K].astype(I32), w[:, 0:TOP_K]


def _outproj_ln_router_kernel(o_ref, w_ref, x_ref, g_ref, b_ref, wrh_ref, wrl_ref, br_ref,
                              hn_ref, id_ref, wt_ref):
    y = _dot(o_ref[...], w_ref[...])
    hn = _layer_norm(DN_ALPHA * x_ref[...] + y, g_ref[...], b_ref[...])
    hn_ref[...] = hn
    hi, lo = _split_bf16(hn)
    logits = (_dot(hi, wrh_ref[...]) + _dot(lo, wrh_ref[...]) + _dot(hi, wrl_ref[...])) + br_ref[...]
    id_ref[...], wt_ref[...] = _route_rows(logits)


def _outproj_ln_router(o_bf16, w_bf16, x, g, b, wr_hi, wr_lo, br):
    m, k = o_bf16.shape
    d = w_bf16.shape[1]
    row = lambda i: (i, 0)
    fix = lambda i: (0, 0)
    return pl.pallas_call(
        _outproj_ln_router_kernel,
        grid=(m // TM,),
        in_specs=[pl.BlockSpec((TM, k), row), pl.BlockSpec((k, d), fix), pl.BlockSpec((TM, d), row),
                  pl.BlockSpec((1, d), fix), pl.BlockSpec((1, d), fix),
                  pl.BlockSpec((d, ROUTER_PAD), fix), pl.BlockSpec((d, ROUTER_PAD), fix),
                  pl.BlockSpec((1, ROUTER_PAD), fix)],
        out_specs=[pl.BlockSpec((TM, d), row), pl.BlockSpec((TM, TOP_K), row), pl.BlockSpec((TM, TOP_K), row)],
        out_shape=[jax.ShapeDtypeStruct((m, d), F32), jax.ShapeDtypeStruct((m, TOP_K), I32),
                   jax.ShapeDtypeStruct((m, TOP_K), F32)],
        compiler_params=_cparams(("parallel",)),
        name="outproj_ln_router",
    )(o_bf16, w_bf16, x, g, b, wr_hi, wr_lo, br)


def _experts_kernel(be_ref, nu_ref, x_ref, wg_ref, wu_ref, wd_ref, o_ref, wg_s, wu_s, wd_s):
    i = pl.program_id(0)
    prev = be_ref[jnp.maximum(i - 1, 0)]

    @pl.when((i == 0) | (be_ref[i] != prev))
    def _():
        wg_s[...] = wg_ref[...].astype(BF16)
        wu_s[...] = wu_ref[...].astype(BF16)
        wd_s[...] = wd_ref[...].astype(BF16)

    @pl.when(i < nu_ref[0])
    def _():
        x = x_ref[...].astype(BF16)
        a = _dot(x, wg_s[...])
        u = _dot(x, wu_s[...])
        hid = (a * jax.nn.sigmoid(a)) * u
        o_ref[...] = _dot(hid.astype(BF16), wd_s[...])

    @pl.when(i >= nu_ref[0])
    def _():
        o_ref[...] = jnp.zeros_like(o_ref)


def _experts(x_sorted, blk_expert, n_used, w_gate, w_up, w_down, layer):
    n_rows, d = x_sorted.shape
    n_blocks = n_rows // TM
    de = w_gate.shape[-1]
    wsel = lambda i, be, nu: (layer, be[i], 0, 0)
    return pl.pallas_call(
        _experts_kernel,
        grid_spec=pltpu.PrefetchScalarGridSpec(
            num_scalar_prefetch=2,
            grid=(n_blocks,),
            in_specs=[pl.BlockSpec((TM, d), lambda i, be, nu: (i, 0)),
                      pl.BlockSpec((None, None, d, de), wsel),
                      pl.BlockSpec((None, None, d, de), wsel),
                      pl.BlockSpec((None, None, de, d), wsel)],
            out_specs=pl.BlockSpec((TM, d), lambda i, be, nu: (i, 0)),
            scratch_shapes=[pltpu.VMEM((d, de), BF16), pltpu.VMEM((d, de), BF16), pltpu.VMEM((de, d), BF16)]),
        out_shape=jax.ShapeDtypeStruct((n_rows, d), F32),
        compiler_params=_cparams(("arbitrary",)),
        name="experts",
    )(blk_expert, n_used, x_sorted, w_gate, w_up, w_down)


def _combine_ln_kernel(h_ref, o0_ref, o1_ref, w_ref, g_ref, b_ref, y_ref):
    w = w_ref[...]
    z = DN_ALPHA * h_ref[...] + (w[:, 0:1] * o0_ref[...] + w[:, 1:2] * o1_ref[...])
    y_ref[...] = _layer_norm(z, g_ref[...], b_ref[...])


def _combine_ln(h, o0, o1, w, g, b):
    m, d = h.shape
    row = lambda i: (i, 0)
    fix = lambda i: (0, 0)
    return pl.pallas_call(
        _combine_ln_kernel,
        grid=(m // TM,),
        in_specs=[pl.BlockSpec((TM, d), row), pl.BlockSpec((TM, d), row), pl.BlockSpec((TM, d), row),
                  pl.BlockSpec((TM, TOP_K), row), pl.BlockSpec((1, d), fix), pl.BlockSpec((1, d), fix)],
        out_specs=pl.BlockSpec((TM, d), row),
        out_shape=jax.ShapeDtypeStruct((m, d), F32),
        compiler_params=_cparams(("parallel",)),
        name="combine_ln",
    )(h, o0, o1, w, g, b)


def _online_value(q, k, v, add, state, feat_major=False):
    return _online_multi(q, [k], [v], [add], state, feat_major)


def _online_multi(q, ks, vs, adds, state, feat_major=False):
    m_old, l_old, acc = state
    ss = [(_dot(q, k) if feat_major else _dot_nt(q, k)) + add for k, add in zip(ks, adds)]
    m_new = jnp.maximum(m_old, jnp.max(functools.reduce(jnp.maximum, ss), -1, keepdims=True))
    m_use = jnp.maximum(m_new, NEG_FLOOR)
    alpha = jnp.exp(m_old - m_use)
    ps = [jnp.exp(s - m_use) for s in ss]
    l_new = alpha * l_old + jnp.sum(functools.reduce(jnp.add, ps), -1, keepdims=True)
    acc = alpha * acc
    for p, v in zip(ps, vs):
        pb = p.astype(BF16)
        acc = acc + (_dot_nt(pb, v) if feat_major else _dot(pb, v))
    return (m_new, l_new, acc)


def _online_step(q, k, v, add, m_ref, l_ref, acc_ref, idx, feat_major=False):
    m_ref[idx], l_ref[idx], acc_ref[idx] = _online_value(
        q, k, v, add, (m_ref[idx], l_ref[idx], acc_ref[idx]), feat_major)


def _online_heads(q_all, k, v, add, m_ref, acc_ref):
    n_g, n_t, n_c = add.shape
    new = [_online_ones(q_all[h * n_t:(h + 1) * n_t], k, v, add[h], m_ref[h], acc_ref[h]) for h in range(n_g)]
    for h in range(n_g):
        m_ref[h], acc_ref[h] = new[h]


def _online_ones(q, k, v1, add, m_old, acc):
    s = _dot_nt(q, k) + add
    m_new = jnp.maximum(m_old, jnp.max(s, -1, keepdims=True))
    m_use = jnp.maximum(m_new, NEG_FLOOR)
    alpha = jnp.exp(m_old - m_use)
    p = jnp.exp(s - jnp.concatenate([m_use] * (s.shape[-1] // LANE), axis=1))
    return m_new, alpha * acc + _dot(p.astype(BF16), v1)


def _masked_softmax(s, mask):
    l = jnp.where(mask, s, NEG)
    m = jnp.max(l, -1, keepdims=True)
    e = jnp.where(mask, jnp.exp(l - m), 0.0)
    return e * (1.0 / jnp.maximum(jnp.sum(e, -1, keepdims=True), TINY))


def _topk_mask(score, blk, n_blk):
    rank = jnp.zeros(score.shape, F32)
    for j in range(n_blk):
        col = score[:, j:j + 1]
        beats = (col > score) | ((col == score) & (j < blk))
        rank = rank + jnp.where(beats, 1.0, 0.0)
    return rank < float(N_SEL)


def _nsa_prompt_kernel(far_ref, q_ref, kc_ref, vc_ref, ks_ref, vs_ref, kw_ref, vw_ref, g_ref, b0_ref, b1_ref,
                       ex_ref, ov_ref, o_ref, msk_ref, oc_ref, ms_ref, as_ref, mw_ref, aw_ref,
                       *, n_tiles, n_cmp, n_blk):
    g = pl.program_id(1)
    t = pl.program_id(2)
    row = lax.broadcasted_iota(I32, (TQ, 1), 0)
    q_pos = t * TQ + row

    kc = kc_ref[0, 0]
    vc = vc_ref[0, 0]
    ncp = kc.shape[0]
    n_idx = lax.broadcasted_iota(I32, (1, ncp), 1)
    cmask = (CMP_STRIDE * n_idx + (CMP_LEN - 1) <= q_pos) & (n_idx < n_cmp)
    q_all = q_ref[0].reshape(NSA_G * TQ, HEAD_DIM)
    p = _masked_softmax(_dot_nt(q_all, kc).reshape(NSA_G, TQ, ncp), cmask[None])
    oc_ref[...] = _dot(p.reshape(NSA_G * TQ, ncp).astype(BF16), vc).reshape(NSA_G, TQ, HEAD_DIM)
    p_sum = jnp.sum(p, axis=0)
    hi, lo = _split_bf16(p_sum)
    imp = _dot(hi, ov_ref[...]) + _dot(lo, ov_ref[...])

    blk = lax.broadcasted_iota(I32, (1, n_blk), 1)
    cur = jnp.right_shift(q_pos, 6)
    visible = blk <= cur
    forced = (blk == 0) | (blk == cur) | (blk == cur - 1)
    score = jnp.where(visible, jnp.where(forced, FORCED, imp), -1.0)
    sel = visible & _topk_mask(score, blk, n_blk)
    mfull = _dot(jnp.where(sel, 1.0, 0.0).astype(BF16), ex_ref[...])
    for c in range(n_tiles):
        msk_ref[c] = mfull[:, c * TQ:(c + 1) * TQ]

    ms_ref[...] = jnp.full(ms_ref.shape, NEG, F32)
    mw_ref[...] = jnp.full(mw_ref.shape, NEG, F32)
    as_ref[...] = jnp.zeros(as_ref.shape, F32)
    aw_ref[...] = jnp.zeros(aw_ref.shape, F32)
    head = lax.broadcasted_iota(I32, (NSA_G, 1, 1), 0)
    far = jnp.zeros((NSA_G, 1, 1), F32)
    for h in range(NSA_G):
        far = jnp.where(head == h, far_ref[g * NSA_G + h], far)

    def sel_chunk(c, bias):
        mk = msk_ref[c] > 0.5
        _online_heads(q_all, ks_ref[0, 0, c], vs_ref[0, 0, c], jnp.where(mk[None], bias, NEG), ms_ref, as_ref)

    def far_body(c, carry):
        sel_chunk(c, far)
        return carry

    lax.fori_loop(0, jnp.maximum(t - 1, 0), far_body, 0)

    @pl.when(t >= 1)
    def _():
        sel_chunk(t - 1, b1_ref[...])

    sel_chunk(t, b0_ref[...])

    def win_chunk(c, add):
        _online_heads(q_all, kw_ref[0, 0, c], vw_ref[0, 0, c], add, mw_ref, aw_ref)

    win_chunk(t, b0_ref[...])

    @pl.when(t >= 1)
    def _():
        win_chunk(t - 1, b1_ref[...])

    @pl.when(t >= 2)
    def _():
        inside = lax.broadcasted_iota(I32, (TQ, TQ), 1) > lax.broadcasted_iota(I32, (TQ, TQ), 0)
        win_chunk(t - 2, jnp.where(inside[None], far, NEG))

    gates = jax.nn.sigmoid(g_ref[0, 0])
    for h in range(NSA_G):
        a_s, a_w = as_ref[h], aw_ref[h]
        o_s = a_s[:, :HEAD_DIM] * (1.0 / jnp.maximum(a_s[:, HEAD_DIM:HEAD_DIM + 1], TINY))
        o_w = a_w[:, :HEAD_DIM] * (1.0 / jnp.maximum(a_w[:, HEAD_DIM:HEAD_DIM + 1], TINY))
        o = (gates[:, 3 * h:3 * h + 1] * oc_ref[h] + gates[:, 3 * h + 1:3 * h + 2] * o_s
             + gates[:, 3 * h + 2:3 * h + 3] * o_w)
        o_ref[0, h] = o.astype(o_ref.dtype)


def _nsa_prompt_attention(q_hm, kc, vc, ks, vs, kw, vw, gates, b0, b1, far, expand, overlap):
    b, _, l, hd = q_hm.shape
    n_tiles = l // TQ
    ncp = kc.shape[2]
    n_blk = l // SEL_BLOCK
    k_spec = pl.BlockSpec((1, 1, n_tiles, TQ, hd), lambda bi, g, t, far: (bi, g, 0, 0, 0))
    v_spec = pl.BlockSpec((1, 1, n_tiles, TQ, 2 * hd), lambda bi, g, t, far: (bi, g, 0, 0, 0))
    cmp_spec = pl.BlockSpec((1, 1, ncp, hd), lambda bi, g, t, far: (bi, g, 0, 0))
    bias_spec = pl.BlockSpec((NSA_G, TQ, TQ), lambda bi, g, t, far: (g, 0, 0))
    kern = functools.partial(_nsa_prompt_kernel, n_tiles=n_tiles, n_cmp=ncp - 1, n_blk=n_blk)
    state = lambda: pltpu.VMEM((NSA_G, TQ, LANE), F32)
    return pl.pallas_call(
        kern,
        grid_spec=pltpu.PrefetchScalarGridSpec(
            num_scalar_prefetch=1,
            grid=(b, NSA_KV, n_tiles),
            in_specs=[pl.BlockSpec((1, NSA_G, TQ, hd), lambda bi, g, t, far: (bi, g, t, 0)),
                      cmp_spec, cmp_spec, k_spec, v_spec, k_spec, v_spec,
                      pl.BlockSpec((1, 1, TQ, 3 * NSA_G), lambda bi, g, t, far: (bi, g, t, 0)),
                      bias_spec, bias_spec,
                      pl.BlockSpec((n_blk, l), lambda bi, g, t, far: (0, 0)),
                      pl.BlockSpec((ncp, n_blk), lambda bi, g, t, far: (0, 0))],
            out_specs=pl.BlockSpec((1, NSA_G, TQ, hd), lambda bi, g, t, far: (bi, g, t, 0)),
            scratch_shapes=[pltpu.VMEM((n_tiles, TQ, TQ), F32), pltpu.VMEM((NSA_G, TQ, hd), F32),
                            state(), state(), state(), state()]),
        out_shape=jax.ShapeDtypeStruct(q_hm.shape, BF16),
        compiler_params=_cparams(("parallel", "parallel", "arbitrary")),
        name="nsa_prompt_attention",
    )(far, q_hm, kc, vc, ks, vs, kw, vw, gates, b0, b1, expand, overlap)


def _swa_prompt_kernel(sink_ref, q_ref, k_ref, v_ref, b0_ref, b1_ref, o_ref, m_ref, a_ref):
    g = pl.program_id(1)
    t = pl.program_id(2)
    lane = lax.broadcasted_iota(I32, (TQ, LANE), 1)
    for h in range(SWA_G):
        m_ref[h] = jnp.full((TQ, LANE), sink_ref[g * SWA_G + h], F32)
        a_ref[h] = jnp.where(lane >= HEAD_DIM, 1.0, 0.0)
    q_all = q_ref[0].reshape(SWA_G * TQ, HEAD_DIM)

    def chunk(c, bias_ref):
        _online_heads(q_all, k_ref[0, 0, c], v_ref[0, 0, c], bias_ref[...], m_ref, a_ref)

    chunk(t, b0_ref)

    @pl.when(t >= 1)
    def _():
        chunk(t - 1, b1_ref)

    for h in range(SWA_G):
        a = a_ref[h]
        o_ref[0, h] = (a[:, :HEAD_DIM] * (1.0 / a[:, HEAD_DIM:HEAD_DIM + 1])).astype(o_ref.dtype)


def _swa_prompt_attention(q_hm, k, v, b0, b1, sinks):
    b, _, l, hd = q_hm.shape
    n_tiles = l // TQ
    k_spec = pl.BlockSpec((1, 1, n_tiles, TQ, hd), lambda bi, g, t, s: (bi, g, 0, 0, 0))
    v_spec = pl.BlockSpec((1, 1, n_tiles, TQ, 2 * hd), lambda bi, g, t, s: (bi, g, 0, 0, 0))
    bias_spec = pl.BlockSpec((SWA_G, TQ, TQ), lambda bi, g, t, s: (g, 0, 0))
    return pl.pallas_call(
        _swa_prompt_kernel,
        grid_spec=pltpu.PrefetchScalarGridSpec(
            num_scalar_prefetch=1,
            grid=(b, SWA_KV, n_tiles),
            in_specs=[pl.BlockSpec((1, SWA_G, TQ, hd), lambda bi, g, t, s: (bi, g, t, 0)),
                      k_spec, v_spec, bias_spec, bias_spec],
            out_specs=pl.BlockSpec((1, SWA_G, TQ, hd), lambda bi, g, t, s: (bi, g, t, 0)),
            scratch_shapes=[pltpu.VMEM((SWA_G, TQ, LANE), F32), pltpu.VMEM((SWA_G, TQ, LANE), F32)]),
        out_shape=jax.ShapeDtypeStruct(q_hm.shape, BF16),
        compiler_params=_cparams(("parallel", "parallel", "arbitrary")),
        name="swa_prompt_attention",
    )(sinks, q_hm, k, v, b0, b1)


def _compress_kernel(pt_ref, *refs, npg, n_steps, feat_major):
    page_refs = refs[:npg]
    wbd_ref, b1_ref, w2_ref, kc_ref, vc_ref, x_s, tp_s = refs[npg:]
    s = pl.program_id(1)
    cpp = PAGE // CMP_STRIDE
    n_q = x_s.shape[1]
    for ip in range(npg // 2):
        base = pl.multiple_of((s * npg + 2 * ip) * cpp, 2 * cpp)
        if feat_major:
            for pi in range(2):
                for c in range(n_q):
                    tp_s[pi, c] = page_refs[2 * ip + pi][c * LANE:(c + 1) * LANE, :].T
        for j in range(CMP_STRIDE):
            for c in range(n_q):
                if feat_major:
                    halves = [tp_s[pi, c, pl.ds(j, cpp, stride=CMP_STRIDE), :] for pi in range(2)]
                else:
                    halves = [page_refs[2 * ip + pi][pl.ds(j * n_q + c, cpp, stride=CMP_STRIDE * n_q), :]
                              for pi in range(2)]
                x_s[j, c, pl.ds(base, 2 * cpp), :] = jnp.concatenate(halves, axis=0).astype(BF16)

    @pl.when(s == n_steps - 1)
    def _():
        n_chunks = x_s.shape[2]
        rows = lax.broadcasted_iota(I32, (n_chunks, 1), 0)
        for kv, out_ref in ((0, kc_ref), (1, vc_ref)):
            for gp in range(NSA_KV // 2):
                c = kv * (NSA_KV // 2) + gp
                y = jnp.zeros((n_chunks, 2 * LANE), F32)
                for jj in range(CMP_STRIDE // 2):
                    lhs = jnp.concatenate([x_s[2 * jj, c], x_s[2 * jj + 1, c]], axis=1)
                    y = y + _dot(lhs, wbd_ref[kv, jj])
                pre = y[:, :LANE] + pltpu.roll(y[:, LANE:], n_chunks - 1, 0) + b1_ref[kv]
                hid = jax.nn.gelu(pre)
                out = _dot(hid.astype(BF16), w2_ref[kv])
                out = jnp.where(rows < n_chunks - 1, out, 0.0)
                out_ref[0, :, gp * LANE:(gp + 1) * LANE] = out.astype(out_ref.dtype)


def _compress(pages, a, page_table, wbd, b1t, w2bd, feat_major):
    nseq, n_pages = page_table.shape
    npg = min(NPG_MAX, n_pages)
    n_steps = n_pages // npg
    n_chunks = n_pages * PAGE // CMP_STRIDE
    blk = pages.shape[2:]
    n_q = 2 * NSA_KV * HEAD_DIM // LANE

    def page_spec(i):
        return pl.BlockSpec((None, None) + blk, lambda b, s, pt: (a, pt[b, s * npg + i], 0, 0))

    out_spec = pl.BlockSpec((1, n_chunks, NSA_KV * HEAD_DIM), lambda b, s, pt: (b, 0, 0))
    kern = functools.partial(_compress_kernel, npg=npg, n_steps=n_steps, feat_major=feat_major)
    fix3 = lambda b, s, pt: (0, 0, 0)
    return pl.pallas_call(
        kern,
        grid_spec=pltpu.PrefetchScalarGridSpec(
            num_scalar_prefetch=1,
            grid=(nseq, n_steps),
            in_specs=[page_spec(i) for i in range(npg)] + [
                pl.BlockSpec(wbd.shape, lambda b, s, pt: (0, 0, 0, 0)),
                pl.BlockSpec(b1t.shape, fix3), pl.BlockSpec(w2bd.shape, fix3)],
            out_specs=[out_spec, out_spec],
            scratch_shapes=[pltpu.VMEM((CMP_STRIDE, n_q, n_chunks, LANE), BF16),
                            pltpu.VMEM((2, n_q, PAGE, LANE), F32)]),
        out_shape=[jax.ShapeDtypeStruct((nseq, n_chunks, NSA_KV * HEAD_DIM), BF16)] * 2,
        compiler_params=_cparams(("parallel", "arbitrary")),
        name="compress",
    )(page_table, *([pages] * npg), wbd, b1t, w2bd)


def _diag_blocks(o, n_groups, rows_per_group):
    return jnp.concatenate(
        [o[g * rows_per_group:(g + 1) * rows_per_group, g * HEAD_DIM:(g + 1) * HEAD_DIM] for g in range(n_groups)],
        axis=0)


def _nsa_sample_kernel(pt_ref, *refs, npg, n_steps, n_cmp, n_blk):
    page_refs = refs[:npg]
    (q_ref, kc_ref, vc_ref, tails_ref, tailw_ref, win_ref, g_ref, farc_ref, blast_ref, btail_ref, bwin_ref,
     cpos_ref, hsum_ref, ov_ref, o_ref, msk_ref, tsel_ref, oc_ref, m_ref, l_ref, a_ref) = refs[npg:]
    s = pl.program_id(1)
    q = q_ref[0]
    n_rows = q.shape[0]
    gw = NSA_KV * HEAD_DIM
    bps = 2 * npg

    @pl.when(s == 0)
    def _():
        kc = kc_ref[0]
        ncp = kc.shape[0]
        n_idx = lax.broadcasted_iota(I32, (1, ncp), 1)
        cmask = (CMP_STRIDE * n_idx + (CMP_LEN - 1) <= cpos_ref[...]) & (n_idx < n_cmp)
        p = _masked_softmax(_dot_nt(q, kc), cmask)
        oc_ref[...] = _dot(p.astype(BF16), vc_ref[0])
        hi, lo = _split_bf16(p)
        p_grp = _dot(hsum_ref[...], hi) + _dot(hsum_ref[...], lo)
        hi, lo = _split_bf16(p_grp)
        imp = _dot(hi, ov_ref[...]) + _dot(lo, ov_ref[...])
        nbp = imp.shape[1]
        blk = lax.broadcasted_iota(I32, (1, nbp), 1)
        cur = n_blk - 1
        visible = blk <= cur
        forced = (blk == 0) | (blk == cur) | (blk == cur - 1)
        score = jnp.where(visible, jnp.where(forced, FORCED, imp), -1.0)
        sel = jnp.where(visible & _topk_mask(score, blk, n_blk), 1.0, 0.0)
        for st in range(n_steps):
            msk_ref[st, :, 0:bps] = sel[:, st * bps:(st + 1) * bps]
        tsel_ref[...] = sel[:, n_blk - 1:n_blk]
        m_ref[...] = jnp.full((n_rows, 1), NEG, F32)
        l_ref[...] = jnp.zeros((n_rows, 1), F32)
        a_ref[...] = jnp.zeros((n_rows, gw), F32)

    ms = msk_ref[s]
    lane = lax.broadcasted_iota(I32, (1, PAGE), 1)
    farc = farc_ref[...]
    for i0 in range(0, npg, PAGES_PER_STEP):
        ks, vs, adds = [], [], []
        for i in range(i0, min(i0 + PAGES_PER_STEP, npg)):
            page = page_refs[i]
            ks.append(page[0:gw, :].astype(BF16))
            vs.append(page[gw:2 * gw, :].astype(BF16))
            mk = jnp.where(lane < SEL_BLOCK, ms[:, 2 * i:2 * i + 1], ms[:, 2 * i + 1:2 * i + 2]) > 0.5
            if i == npg - 1:
                bias = jnp.where(s == n_steps - 1, blast_ref[...], farc)
            else:
                bias = farc
            adds.append(jnp.where(mk, bias, NEG))
        m_ref[...], l_ref[...], a_ref[...] = _online_multi(
            q, ks, vs, adds, (m_ref[...], l_ref[...], a_ref[...]), feat_major=True)

    @pl.when(s == n_steps - 1)
    def _():
        tails = tails_ref[0]
        _online_step(q, tails[:, 0:gw], tails[:, gw:2 * gw], jnp.where(tsel_ref[...] > 0.5, btail_ref[...], NEG),
                     m_ref, l_ref, a_ref, slice(None))
        o_s = a_ref[...] * (1.0 / jnp.maximum(l_ref[...], TINY))

        tailw = tailw_ref[0]
        state = (jnp.full((n_rows, 1), NEG, F32), jnp.zeros((n_rows, 1), F32), jnp.zeros((n_rows, gw), F32))
        state = _online_value(q, tailw[:, 0:gw], tailw[:, gw:2 * gw], btail_ref[...], state)
        wl = win_ref.shape[3]
        ks, vs, adds = [], [], []
        for c in range(wl // PAGE):
            rows = win_ref[0, 0, :, c * PAGE:(c + 1) * PAGE]
            ks.append(rows[0:gw, :].astype(BF16))
            vs.append(rows[gw:2 * gw, :].astype(BF16))
            adds.append(bwin_ref[:, c * PAGE:(c + 1) * PAGE])
        state = _online_multi(q, ks, vs, adds, state, feat_major=True)
        o_w = state[2] * (1.0 / jnp.maximum(state[1], TINY))

        gates = jax.nn.sigmoid(g_ref[0])
        o = gates[:, 0:1] * oc_ref[...] + gates[:, 1:2] * o_s + gates[:, 2:3] * o_w
        o_ref[0] = _diag_blocks(o, NSA_KV, n_rows // NSA_KV)


def _nsa_sample_attention(q_bd, kc, vc, cache4, a, page_table, tail_s, tail_w, win4, gates, farc, blast, btail, bwin,
                          cpos, hsum, overlap, n_blk):
    bs, n_rows, gw = q_bd.shape
    n_pages = page_table.shape[1]
    npg = min(NPG_MAX, n_pages)
    n_steps = n_pages // npg
    ncp = kc.shape[1]
    wl = win4.shape[3]

    def page_spec(i):
        return pl.BlockSpec((None, None, 2 * gw, PAGE), lambda b, s, pt: (a, pt[b, s * npg + i], 0, 0))

    per_b3 = lambda b, s, pt: (b, 0, 0)
    fix2 = lambda b, s, pt: (0, 0)
    full2 = lambda arr: pl.BlockSpec(arr.shape, fix2)
    kern = functools.partial(_nsa_sample_kernel, npg=npg, n_steps=n_steps, n_cmp=ncp - 1, n_blk=n_blk)
    return pl.pallas_call(
        kern,
        grid_spec=pltpu.PrefetchScalarGridSpec(
            num_scalar_prefetch=1,
            grid=(bs, n_steps),
            in_specs=[page_spec(i) for i in range(npg)] + [
                pl.BlockSpec((1, n_rows, gw), per_b3),
                pl.BlockSpec((1, ncp, gw), per_b3), pl.BlockSpec((1, ncp, gw), per_b3),
                pl.BlockSpec((1, PAGE, 2 * gw), per_b3), pl.BlockSpec((1, PAGE, 2 * gw), per_b3),
                pl.BlockSpec((1, 1, 2 * gw, wl), lambda b, s, pt: (a, b, 0, 0)),
                pl.BlockSpec((1, n_rows, 3), per_b3),
                full2(farc), full2(blast), full2(btail), full2(bwin), full2(cpos), full2(hsum), full2(overlap)],
            out_specs=pl.BlockSpec((1, n_rows, HEAD_DIM), per_b3),
            scratch_shapes=[pltpu.VMEM((n_steps, n_rows, LANE), F32), pltpu.VMEM((n_rows, 1), F32),
                            pltpu.VMEM((n_rows, gw), F32), pltpu.VMEM((n_rows, 1), F32),
                            pltpu.VMEM((n_rows, 1), F32), pltpu.VMEM((n_rows, gw), F32)]),
        out_shape=jax.ShapeDtypeStruct((bs, n_rows, HEAD_DIM), F32),
        compiler_params=_cparams(("parallel", "arbitrary")),
        name="nsa_sample_attention",
    )(page_table, *([cache4] * npg), q_bd, kc, vc, tail_s, tail_w, win4, gates, farc, blast, btail, bwin,
      cpos, hsum, overlap)


def _swa_sample_kernel(q_ref, buf_ref, tail_ref, sink_ref, bbuf_ref, btail_ref, o_ref):
    q = q_ref[0]
    n_rows = q.shape[0]
    gw = SWA_KV * HEAD_DIM
    state = (sink_ref[...], jnp.ones((n_rows, 1), F32), jnp.zeros((n_rows, gw), F32))
    tail = tail_ref[0]
    state = _online_value(q, tail[:, 0:gw], tail[:, gw:2 * gw], btail_ref[...], state)
    wl = buf_ref.shape[3]
    for c in range(wl // PAGE):
        rows = buf_ref[0, 0, :, c * PAGE:(c + 1) * PAGE]
        state = _online_value(q, rows[0:gw, :].astype(BF16), rows[gw:2 * gw, :].astype(BF16),
                              bbuf_ref[:, c * PAGE:(c + 1) * PAGE], state, feat_major=True)
    o = state[2] * (1.0 / state[1])
    o_ref[0] = _diag_blocks(o, SWA_KV, n_rows // SWA_KV)


def _swa_sample_attention(q_bd, buf4, a, tail, sinkc, bbuf, btail):
    bs, n_rows, gw = q_bd.shape
    wl = buf4.shape[3]
    per_b3 = lambda b: (b, 0, 0)
    full2 = lambda arr: pl.BlockSpec(arr.shape, lambda b: (0, 0))
    return pl.pallas_call(
        _swa_sample_kernel,
        grid=(bs,),
        in_specs=[pl.BlockSpec((1, n_rows, gw), per_b3),
                  pl.BlockSpec((1, 1, 2 * gw, wl), lambda b: (a, b, 0, 0)),
                  pl.BlockSpec((1, PAGE, 2 * gw), per_b3),
                  full2(sinkc), full2(bbuf), full2(btail)],
        out_specs=pl.BlockSpec((1, n_rows, HEAD_DIM), per_b3),
        out_shape=jax.ShapeDtypeStruct((bs, n_rows, HEAD_DIM), F32),
        compiler_params=_cparams(("parallel",)),
        name="swa_sample_attention",
    )(q_bd, buf4, tail, sinkc, bbuf, btail)


def _rel_bucket(dist):
    n = jnp.maximum(dist, 0)
    nf = jnp.maximum(n, 1).astype(F32)
    far = REL_MAX_EXACT + (jnp.log(nf / REL_MAX_EXACT) / math.log(REL_MAX_DIST / REL_MAX_EXACT)
                           * (REL_BUCKETS - REL_MAX_EXACT)).astype(I32)
    return jnp.where(n < REL_MAX_EXACT, n, jnp.minimum(far, REL_BUCKETS - 1))


def _bias_of(rel_bias, dist):
    return jnp.moveaxis(rel_bias[_rel_bucket(dist)], -1, 0)


def _prompt_bias_tiles(rel_bias, window):
    qi = jnp.arange(TQ)[:, None]
    kj = jnp.arange(TQ)[None, :]
    d0 = qi - kj
    d1 = TQ + qi - kj
    b0 = jnp.where((d0 >= 0) & (d0 < window), _bias_of(rel_bias, d0), NEG)
    b1 = jnp.where(d1 < window, _bias_of(rel_bias, d1), NEG)
    return b0.astype(F32), b1.astype(F32)


def _sample_bias_tiles(rel_bias, t, wl, window):
    n_rows = N_HEADS * t
    head = jnp.arange(n_rows) // t
    q = (jnp.arange(n_rows) % t)[:, None]
    kj = jnp.arange(PAGE)[None, :]
    d_tail = q - kj
    tail_all = _bias_of(rel_bias, d_tail)[head, jnp.arange(n_rows)]
    btail = jnp.where((d_tail >= 0) & (kj < t), tail_all, NEG)
    i = jnp.arange(wl)[None, :]
    d_buf = wl + q - i
    buf_all = _bias_of(rel_bias, d_buf)[head, jnp.arange(n_rows)]
    bbuf = jnp.where(d_buf < window, buf_all, NEG)
    return btail.astype(F32), bbuf.astype(F32), head, q


def _block_diag_q(q, n_kv, t):
    bs = q.shape[0]
    grp = N_HEADS // n_kv
    q5 = q.reshape(bs, t, n_kv, grp, HEAD_DIM).transpose(0, 2, 3, 1, 4)
    eye = jnp.eye(n_kv, dtype=q.dtype)
    qbd = jnp.einsum('bghqd,gk->bghqkd', q5, eye)
    return qbd.reshape(bs, N_HEADS * t, n_kv * HEAD_DIM).astype(BF16)


def _rows_to_tokens(o, t):
    bs = o.shape[0]
    return o.reshape(bs, N_HEADS, t, HEAD_DIM).transpose(0, 2, 1, 3).reshape(bs * t, N_HEADS * HEAD_DIM)


def _head_major(x, n_heads):
    b, l, _ = x.shape
    return x.reshape(b, l, n_heads, HEAD_DIM).transpose(0, 2, 1, 3).astype(BF16)


def _with_ones(v):
    return jnp.concatenate([v, jnp.ones_like(v)], axis=-1)


def _pad_tail(rows, t):
    return jnp.pad(rows, ((0, 0), (0, PAGE - t), (0, 0))).astype(BF16)


def _compress_weights(w1, b1, w2):
    w = w1.reshape(2, 2, CMP_STRIDE // 2, 2, HEAD_DIM, HEAD_DIM)
    eye = jnp.eye(2, dtype=w1.dtype)
    wbd = jnp.einsum('kpjlde,gh->kjlgdphe', w, eye).reshape(2, CMP_STRIDE // 2, 4 * HEAD_DIM, 4 * HEAD_DIM)
    b1t = jnp.tile(b1, (1, 2)).reshape(2, 1, 2 * HEAD_DIM)
    w2bd = jnp.einsum('kde,gh->kgdhe', w2, eye).reshape(2, 2 * HEAD_DIM, 2 * HEAD_DIM)
    return wbd.astype(BF16), b1t.astype(F32), w2bd.astype(BF16)


def _overlap(n_cmp_rows, n_blk_cols):
    n = CMP_STRIDE * jnp.arange(n_cmp_rows)[:, None]
    j = jnp.arange(n_blk_cols)[None, :]
    return ((n < SEL_BLOCK * (j + 1)) & (n + CMP_LEN > SEL_BLOCK * j)).astype(BF16)


def _dispatch_plan(ids):
    n = ids.shape[0]
    order = jnp.argsort(ids)
    sid = ids[order]
    counts = jnp.bincount(ids, length=N_EXPERTS)
    padded = (counts + TM - 1) // TM * TM
    pad_end = jnp.cumsum(padded)
    pad_start = pad_end - padded
    raw_start = jnp.cumsum(counts) - counts
    dest = (pad_start[sid] + jnp.arange(n) - raw_start[sid]).astype(I32)
    n_blocks = -(-n // TM) + N_EXPERTS
    src = jnp.zeros((n_blocks * TM,), I32).at[dest].set((order // TOP_K).astype(I32))
    blk_start = jnp.arange(n_blocks) * TM
    blk_expert = jnp.minimum(jnp.sum(pad_end[None, :] <= blk_start[:, None], axis=1), N_EXPERTS - 1).astype(I32)
    n_used = (pad_end[-1:] // TM).astype(I32)
    pos = jnp.zeros((n,), I32).at[order].set(dest)
    return src, blk_expert, n_used, pos


def _moe(hn, ids, w, w_gate, w_up, w_down, layer, g, b):
    src, blk_expert, n_used, pos = _dispatch_plan(ids.reshape(-1))
    out = _experts(hn[src], blk_expert, n_used, w_gate, w_up, w_down, layer)
    return _combine_ln(hn, out[pos[0::2]], out[pos[1::2]], w, g, b)


def kernel(x_prompt, x_sample, cache_nsa_cmp, cache_nsa_sel, state_nsa_win, state_swa_win, page_table, rel_bias, nsa_w_in, nsa_b_in, nsa_cmp_w1, nsa_cmp_b1, nsa_cmp_w2, nsa_w_out, swa_w_in, swa_sinks, swa_w_out, moe_w_group, moe_b_group, moe_w_expert, moe_b_expert, moe_w_gate, moe_w_up, moe_w_down, ln_g, ln_b):
    bp, l, d = x_prompt.shape
    bs, t, _ = x_sample.shape
    n_pages = page_table.shape[1]
    past = n_pages * PAGE
    ntp = bp * l
    hq = N_HEADS * HEAD_DIM
    gw_n = NSA_KV * HEAD_DIM
    gw_s = SWA_KV * HEAD_DIM
    assert l % TQ == 0 and NSA_WINDOW == 2 * TQ and SWA_WINDOW <= TQ and TQ >= REL_MAX_DIST
    assert ntp % TM == 0 and t <= CMP_LEN - 1 and t <= SEL_BLOCK and n_pages % 2 == 0
    nsa_wl = state_nsa_win.shape[2]
    swa_wl = state_swa_win.shape[2]
    assert nsa_wl % PAGE == 0 and swa_wl % PAGE == 0
    nts = bs * t
    row_pad = -(ntp + nts) % TM

    x = jnp.concatenate([x_prompt.reshape(ntp, d), x_sample.reshape(nts, d), jnp.zeros((row_pad, d), F32)], axis=0)

    far = rel_bias[REL_BUCKETS - 1].astype(F32)
    nb0, nb1 = _prompt_bias_tiles(rel_bias, NSA_WINDOW)
    sb0, sb1 = _prompt_bias_tiles(rel_bias, SWA_WINDOW)
    n_blk_p = l // SEL_BLOCK
    expand = (jnp.arange(l)[None, :] // SEL_BLOCK == jnp.arange(n_blk_p)[:, None]).astype(BF16)
    ov_p = _overlap(l // CMP_STRIDE, n_blk_p)
    n_blk_s = past // SEL_BLOCK + 1
    nbp_s = -(-n_blk_s // LANE) * LANE
    ov_s = _overlap(past // CMP_STRIDE, nbp_s)
    n_rows = N_HEADS * t
    btail, nbwin, head, qrow = _sample_bias_tiles(rel_bias, t, nsa_wl, NSA_WINDOW)
    _, sbbuf, _, _ = _sample_bias_tiles(rel_bias, t, swa_wl, SWA_WINDOW)
    farc = far[head][:, None]
    d_last = PAGE + qrow - jnp.arange(PAGE)[None, :]
    blast = _bias_of(rel_bias, d_last)[head, jnp.arange(n_rows)].astype(F32)
    cpos = (past + qrow).astype(I32)
    grp_rows = NSA_G * t
    r = jnp.arange(n_rows)
    hsum = ((r[:, None] // grp_rows == r[None, :] // grp_rows) & (r[:, None] % t == r[None, :] % t)).astype(BF16)
    prompt_pages = jnp.arange(ntp // PAGE, dtype=I32).reshape(bp, l // PAGE)

    def feat_major(z):
        return jnp.transpose(z, (0, 1, 3, 4, 5, 2)).reshape(z.shape[0], z.shape[1], -1, z.shape[2])

    cache_cmp4 = feat_major(cache_nsa_cmp)
    cache_sel4 = feat_major(cache_nsa_sel)
    nsa_win4 = feat_major(state_nsa_win)
    swa_win4 = feat_major(state_swa_win)

    outs = {k: [] for k in ('cmp_p', 'cmp_s', 'sel_p', 'sel_s', 'nwin_p', 'nwin_s', 'swin_p', 'swin_s')}
    for layer in range(DEPTH):
        j = layer // 2
        if layer % 2 == 0:
            nsa_in = nsa_w_in.shape[-1]
            n_pad = -(-nsa_in // LANE) * LANE
            w_in = jnp.pad(nsa_w_in[j], ((0, 0), (0, n_pad - nsa_in))).astype(BF16)
            b_in = jnp.pad(nsa_b_in[j], (0, n_pad - nsa_in)).reshape(1, n_pad)
            h = _project(x, w_in, b_in)
            hp = h[:ntp].reshape(bp, l, n_pad)
            hs = h[ntp:ntp + nts].reshape(bs, t, n_pad)
            c0, c1, c2, c3 = hq, hq + 2 * gw_n, hq + 4 * gw_n, hq + 6 * gw_n
            wbd, b1t, w2bd = _compress_weights(nsa_cmp_w1[j], nsa_cmp_b1[j], nsa_cmp_w2[j])

            kvc, kvs, kvw = hp[..., c0:c1], hp[..., c1:c2], hp[..., c2:c3]
            kc, vc = _compress(kvc.reshape(1, ntp // PAGE, PAGE * 2 * gw_n // LANE, LANE), 0, prompt_pages,
                               wbd, b1t, w2bd, feat_major=False)
            chunked = lambda z: _head_major(z, NSA_KV).reshape(bp, NSA_KV, l // TQ, TQ, HEAD_DIM)
            gates_p = hp[..., c3:c3 + 3 * N_HEADS].reshape(bp, l, NSA_KV, 3 * NSA_G).transpose(0, 2, 1, 3)
            o_p = _nsa_prompt_attention(
                _head_major(hp[..., :hq] * SCALE, N_HEADS), _head_major(kc, NSA_KV), _head_major(vc, NSA_KV),
                chunked(kvs[..., :gw_n]), _with_ones(chunked(kvs[..., gw_n:])),
                chunked(kvw[..., :gw_n]), _with_ones(chunked(kvw[..., gw_n:])),
                gates_p, nb0, nb1, far, expand, ov_p)
            o_p = o_p.transpose(0, 2, 1, 3).reshape(ntp, hq)
            kv_shape = (2, NSA_KV, HEAD_DIM)
            outs['cmp_p'].append(kvc.reshape(bp, l, *kv_shape))
            outs['sel_p'].append(kvs.reshape(bp, l, *kv_shape))
            wl_p = min(NSA_WINDOW, l)
            outs['nwin_p'].append(kvw[:, l - wl_p:].reshape(bp, wl_p, *kv_shape))

            kcs, vcs = _compress(cache_cmp4, j, page_table, wbd, b1t, w2bd, feat_major=True)
            gates_s = hs[..., c3:c3 + 3 * N_HEADS].reshape(bs, t, N_HEADS, 3).transpose(0, 2, 1, 3)
            o_s = _nsa_sample_attention(
                _block_diag_q(hs[..., :hq] * SCALE, NSA_KV, t), kcs, vcs, cache_sel4, j, page_table,
                _pad_tail(hs[..., c1:c2], t), _pad_tail(hs[..., c2:c3], t), nsa_win4,
                gates_s.reshape(bs, n_rows, 3), farc, blast, btail, nbwin, cpos, hsum, ov_s, n_blk_s)
            outs['cmp_s'].append(hs[..., c0:c1].reshape(bs, t, *kv_shape))
            outs['sel_s'].append(hs[..., c1:c2].reshape(bs, t, *kv_shape))
            outs['nwin_s'].append(jnp.concatenate(
                [state_nsa_win[j][:, t:], hs[..., c2:c3].reshape(bs, t, *kv_shape)], axis=1))
            w_out = nsa_w_out[j]
        else:
            swa_in = swa_w_in.shape[-1]
            h = _project(x, swa_w_in[j].astype(BF16), jnp.zeros((1, swa_in), F32))
            hp = h[:ntp].reshape(bp, l, swa_in)
            hs = h[ntp:ntp + nts].reshape(bs, t, swa_in)
            kp, vp = hp[..., hq:hq + gw_s], hp[..., hq + gw_s:]
            chunked = lambda z: _head_major(z, SWA_KV).reshape(bp, SWA_KV, l // TQ, TQ, HEAD_DIM)
            o_p = _swa_prompt_attention(_head_major(hp[..., :hq] * SCALE, N_HEADS), chunked(kp),
                                        _with_ones(chunked(vp)),
                                        sb0, sb1, swa_sinks[j].astype(F32))
            o_p = o_p.transpose(0, 2, 1, 3).reshape(ntp, hq)
            wl_p = min(SWA_WINDOW, l)
            sw_shape = (SWA_KV, HEAD_DIM)
            outs['swin_p'].append(jnp.stack([kp[:, l - wl_p:].reshape(bp, wl_p, *sw_shape),
                                             vp[:, l - wl_p:].reshape(bp, wl_p, *sw_shape)], axis=2))
            o_s = _swa_sample_attention(_block_diag_q(hs[..., :hq] * SCALE, SWA_KV, t), swa_win4, j,
                                        _pad_tail(hs[..., hq:], t), swa_sinks[j][head][:, None].astype(F32),
                                        sbbuf, btail)
            new_rows = jnp.stack([hs[..., hq:hq + gw_s].reshape(bs, t, *sw_shape),
                                  hs[..., hq + gw_s:].reshape(bs, t, *sw_shape)], axis=2)
            outs['swin_s'].append(jnp.concatenate([state_swa_win[j][:, t:], new_rows], axis=1))
            w_out = swa_w_out[j]

        o = jnp.concatenate([o_p, _rows_to_tokens(o_s, t).astype(BF16), jnp.zeros((row_pad, hq), BF16)], axis=0)
        wr = jnp.pad(jnp.concatenate([moe_w_group[layer], moe_w_expert[layer]], axis=1),
                     ((0, 0), (0, ROUTER_PAD - N_GROUPS - N_EXPERTS)))
        br = jnp.pad(jnp.concatenate([moe_b_group[layer], moe_b_expert[layer]]),
                     (0, ROUTER_PAD - N_GROUPS - N_EXPERTS)).reshape(1, ROUTER_PAD)
        wr_hi, wr_lo = _split_bf16(wr)
        hn, ids, wts = _outproj_ln_router(o, w_out.astype(BF16), x, ln_g[layer, 0].reshape(1, d),
                                          ln_b[layer, 0].reshape(1, d), wr_hi, wr_lo, br)
        x = _moe(hn, ids, wts, moe_w_gate, moe_w_up, moe_w_down, layer,
                 ln_g[layer, 1].reshape(1, d), ln_b[layer, 1].reshape(1, d))

    st = lambda k: jnp.stack(outs[k])
    return (x[:ntp].reshape(bp, l, d), x[ntp:ntp + nts].reshape(bs, t, d), st('cmp_p'), st('cmp_s'), st('sel_p'), st('sel_s'),
            st('nwin_p'), st('nwin_s'), st('swin_p'), st('swin_s'))
```

```python
import functools
import math

import jax
import jax.numpy as jnp
from jax import lax
from jax.experimental import pallas as pl
from jax.experimental.pallas import tpu as pltpu

F32 = jnp.float32
BF16 = jnp.bfloat16
I32 = jnp.int32

D_MODEL = 1024
N_HEADS = 16
HEAD_DIM = 64
NSA_KV = 4
NSA_G = N_HEADS // NSA_KV
CMP_STRIDE = 16
CMP_LEN = 32
SEL_BLOCK = 64
N_SEL = 16
NSA_WINDOW = 512
SWA_KV = 2
SWA_G = N_HEADS // SWA_KV
SWA_WINDOW = 128
REL_BUCKETS = 32
REL_MAX_EXACT = 16
REL_MAX_DIST = 128
N_GROUPS = 4
EPG = 8
N_EXPERTS = N_GROUPS * EPG
TOP_K = 2
D_EXPERT = 512
PAGE = 128
DEPTH = 4
DN_ALPHA = (2 * DEPTH) ** 0.25
LN_EPS = 1e-5
SCALE = HEAD_DIM ** -0.5
NEG = -1e30
NEG_FLOOR = -1e29
TINY = 1e-30
FORCED = 1e4

LANE = 128
TQ = 256
TM = 256
ROUTER_PAD = LANE
NPG_MAX = 16
PAGES_PER_STEP = 4
VMEM_LIMIT = 52 * 1024 * 1024


def _cparams(sem):
    return pltpu.CompilerParams(dimension_semantics=sem, vmem_limit_bytes=VMEM_LIMIT)


def _dot_nt(a, b):
    return lax.dot_general(a, b, (((1,), (1,)), ((), ())), preferred_element_type=F32)


def _dot(a, b):
    return jnp.dot(a, b, preferred_element_type=F32)


def _split_bf16(x):
    hi = x.astype(BF16)
    lo = (x - hi.astype(F32)).astype(BF16)
    return hi, lo


def _proj_kernel(x_ref, w_ref, b_ref, o_ref):
    o_ref[...] = _dot(x_ref[...].astype(BF16), w_ref[...]) + b_ref[...]


def _project(x, w_bf16, bias):
    m, k = x.shape
    n = w_bf16.shape[1]
    return pl.pallas_call(
        _proj_kernel,
        grid=(m // TM,),
        in_specs=[pl.BlockSpec((TM, k), lambda i: (i, 0)),
                  pl.BlockSpec((k, n), lambda i: (0, 0)),
                  pl.BlockSpec((1, n), lambda i: (0, 0))],
        out_specs=pl.BlockSpec((TM, n), lambda i: (i, 0)),
        out_shape=jax.ShapeDtypeStruct((m, n), F32),
        compiler_params=_cparams(("parallel",)),
        name="project",
    )(x, w_bf16, bias)


def _layer_norm(z, g, b):
    mu = jnp.mean(z, -1, keepdims=True)
    zc = z - mu
    var = jnp.mean(zc * zc, -1, keepdims=True)
    return zc * lax.rsqrt(var + LN_EPS) * g + b


def _first_lane_of_max(cand, lanef):
    top = jnp.max(cand, -1, keepdims=True)
    return top, jnp.min(jnp.where(cand == top, lanef, float(cand.shape[-1])), -1, keepdims=True)


def _route_rows(logits):
    lanef = lax.broadcasted_iota(I32, (1, ROUTER_PAD), 1).astype(F32)
    is_g = lanef < float(N_GROUPS)
    lg = jnp.where(is_g, logits, NEG)
    eg = jnp.where(is_g, jnp.exp(lg - jnp.max(lg, -1, keepdims=True)), 0.0)
    g_prob = eg / jnp.sum(eg, -1, keepdims=True)
    g_p, g_i = _first_lane_of_max(jnp.where(is_g, g_prob, -1.0), lanef)
    lo = float(N_GROUPS) + float(EPG) * g_i
    in_grp = (lanef >= lo) & (lanef < lo + float(EPG))
    le = jnp.where(in_grp, logits, NEG)
    ee = jnp.where(in_grp, jnp.exp(le - jnp.max(le, -1, keepdims=True)), 0.0)
    cand = jnp.where(in_grp, ee / jnp.sum(ee, -1, keepdims=True), -1.0)
    p1, i1 = _first_lane_of_max(cand, lanef)
    p2, i2 = _first_lane_of_max(jnp.where(lanef == i1, -1.0, cand), lanef)
    den = p1 + p2
    first = lanef == 0.0
    ids = jnp.where(first, i1, i2) - float(N_GROUPS)
    w = jnp.where(first, p1 / den * g_p, p2 / den * g_p)
    return ids[:, 0:TOP_K].astype(I32), w[:, 0:TOP_K]


def _outproj_ln_router_kernel(o_ref, w_ref, x_ref, g_ref, b_ref, wrh_ref, wrl_ref, br_ref,
                              hn_ref, id_ref, wt_ref):
    y = _dot(o_ref[...], w_ref[...])
    hn = _layer_norm(DN_ALPHA * x_ref[...] + y, g_ref[...], b_ref[...])
    hn_ref[...] = hn
    hi, lo = _split_bf16(hn)
    logits = (_dot(hi, wrh_ref[...]) + _dot(lo, wrh_ref[...]) + _dot(hi, wrl_ref[...])) + br_ref[...]
    id_ref[...], wt_ref[...] = _route_rows(logits)


def _outproj_ln_router(o_bf16, w_bf16, x, g, b, wr_hi, wr_lo, br):
    m, k = o_bf16.shape
    d = w_bf16.shape[1]
    row = lambda i: (i, 0)
    fix = lambda i: (0, 0)
    return pl.pallas_call(
        _outproj_ln_router_kernel,
        grid=(m // TM,),
        in_specs=[pl.BlockSpec((TM, k), row), pl.BlockSpec((k, d), fix), pl.BlockSpec((TM, d), row),
                  pl.BlockSpec((1, d), fix), pl.BlockSpec((1, d), fix),
                  pl.BlockSpec((d, ROUTER_PAD), fix), pl.BlockSpec((d, ROUTER_PAD), fix),
                  pl.BlockSpec((1, ROUTER_PAD), fix)],
        out_specs=[pl.BlockSpec((TM, d), row), pl.BlockSpec((TM, TOP_K), row), pl.BlockSpec((TM, TOP_K), row)],
        out_shape=[jax.ShapeDtypeStruct((m, d), F32), jax.ShapeDtypeStruct((m, TOP_K), I32),
                   jax.ShapeDtypeStruct((m, TOP_K), F32)],
        compiler_params=_cparams(("parallel",)),
        name="outproj_ln_router",
    )(o_bf16, w_bf16, x, g, b, wr_hi, wr_lo, br)


def _experts_kernel(be_ref, nu_ref, x_ref, wg_ref, wu_ref, wd_ref, o_ref, wg_s, wu_s, wd_s):
    i = pl.program_id(0)
    prev = be_ref[jnp.maximum(i - 1, 0)]

    @pl.when((i == 0) | (be_ref[i] != prev))
    def _():
        wg_s[...] = wg_ref[...].astype(BF16)
        wu_s[...] = wu_ref[...].astype(BF16)
        wd_s[...] = wd_ref[...].astype(BF16)

    @pl.when(i < nu_ref[0])
    def _():
        x = x_ref[...].astype(BF16)
        a = _dot(x, wg_s[...])
        u = _dot(x, wu_s[...])
        hid = (a * jax.nn.sigmoid(a)) * u
        o_ref[...] = _dot(hid.astype(BF16), wd_s[...])

    @pl.when(i >= nu_ref[0])
    def _():
        o_ref[...] = jnp.zeros_like(o_ref)


def _experts(x_sorted, blk_expert, n_used, w_gate, w_up, w_down, layer):
    n_rows, d = x_sorted.shape
    n_blocks = n_rows // TM
    de = w_gate.shape[-1]
    wsel = lambda i, be, nu: (layer, be[i], 0, 0)
    return pl.pallas_call(
        _experts_kernel,
        grid_spec=pltpu.PrefetchScalarGridSpec(
            num_scalar_prefetch=2,
            grid=(n_blocks,),
            in_specs=[pl.BlockSpec((TM, d), lambda i, be, nu: (i, 0)),
                      pl.BlockSpec((None, None, d, de), wsel),
                      pl.BlockSpec((None, None, d, de), wsel),
                      pl.BlockSpec((None, None, de, d), wsel)],
            out_specs=pl.BlockSpec((TM, d), lambda i, be, nu: (i, 0)),
            scratch_shapes=[pltpu.VMEM((d, de), BF16), pltpu.VMEM((d, de), BF16), pltpu.VMEM((de, d), BF16)]),
        out_shape=jax.ShapeDtypeStruct((n_rows, d), F32),
        compiler_params=_cparams(("arbitrary",)),
        name="experts",
    )(blk_expert, n_used, x_sorted, w_gate, w_up, w_down)


def _combine_ln_kernel(h_ref, o0_ref, o1_ref, w_ref, g_ref, b_ref, y_ref):
    w = w_ref[...]
    z = DN_ALPHA * h_ref[...] + (w[:, 0:1] * o0_ref[...] + w[:, 1:2] * o1_ref[...])
    y_ref[...] = _layer_norm(z, g_ref[...], b_ref[...])


def _combine_ln(h, o0, o1, w, g, b):
    m, d = h.shape
    row = lambda i: (i, 0)
    fix = lambda i: (0, 0)
    return pl.pallas_call(
        _combine_ln_kernel,
        grid=(m // TM,),
        in_specs=[pl.BlockSpec((TM, d), row), pl.BlockSpec((TM, d), row), pl.BlockSpec((TM, d), row),
                  pl.BlockSpec((TM, TOP_K), row), pl.BlockSpec((1, d), fix), pl.BlockSpec((1, d), fix)],
        out_specs=pl.BlockSpec((TM, d), row),
        out_shape=jax.ShapeDtypeStruct((m, d), F32),
        compiler_params=_cparams(("parallel",)),
        name="combine_ln",
    )(h, o0, o1, w, g, b)


def _online_value(q, k, v, add, state, feat_major=False):
    return _online_multi(q, [k], [v], [add], state, feat_major)


def _online_multi(q, ks, vs, adds, state, feat_major=False):
    m_old, l_old, acc = state
    ss = [(_dot(q, k) if feat_major else _dot_nt(q, k)) + add for k, add in zip(ks, adds)]
    m_new = jnp.maximum(m_old, jnp.max(functools.reduce(jnp.maximum, ss), -1, keepdims=True))
    m_use = jnp.maximum(m_new, NEG_FLOOR)
    alpha = jnp.exp(m_old - m_use)
    ps = [jnp.exp(s - m_use) for s in ss]
    l_new = alpha * l_old + jnp.sum(functools.reduce(jnp.add, ps), -1, keepdims=True)
    acc = alpha * acc
    for p, v in zip(ps, vs):
        pb = p.astype(BF16)
        acc = acc + (_dot_nt(pb, v) if feat_major else _dot(pb, v))
    return (m_new, l_new, acc)


def _online_step(q, k, v, add, m_ref, l_ref, acc_ref, idx, feat_major=False):
    m_ref[idx], l_ref[idx], acc_ref[idx] = _online_value(
        q, k, v, add, (m_ref[idx], l_ref[idx], acc_ref[idx]), feat_major)


def _online_heads(q_all, k, v, add, m_ref, acc_ref):
    n_g, n_t, n_c = add.shape
    new = [_online_ones(q_all[h * n_t:(h + 1) * n_t], k, v, add[h], m_ref[h], acc_ref[h]) for h in range(n_g)]
    for h in range(n_g):
        m_ref[h], acc_ref[h] = new[h]


def _online_ones(q, k, v1, add, m_old, acc):
    s = _dot_nt(q, k) + add
    m_new = jnp.maximum(m_old, jnp.max(s, -1, keepdims=True))
    m_use = jnp.maximum(m_new, NEG_FLOOR)
    alpha = jnp.exp(m_old - m_use)
    p = jnp.exp(s - jnp.concatenate([m_use] * (s.shape[-1] // LANE), axis=1))
    return m_new, alpha * acc + _dot(p.astype(BF16), v1)


def _masked_softmax(s, mask):
    l = jnp.where(mask, s, NEG)
    m = jnp.max(l, -1, keepdims=True)
    e = jnp.where(mask, jnp.exp(l - m), 0.0)
    return e * (1.0 / jnp.maximum(jnp.sum(e, -1, keepdims=True), TINY))


def _topk_mask(score, blk, n_blk):
    rank = jnp.zeros(score.shape, F32)
    for j in range(n_blk):
        col = score[:, j:j + 1]
        beats = (col > score) | ((col == score) & (j < blk))
        rank = rank + jnp.where(beats, 1.0, 0.0)
    return rank < float(N_SEL)


def _nsa_prompt_kernel(far_ref, q_ref, kc_ref, vc_ref, ks_ref, vs_ref, kw_ref, vw_ref, g_ref, b0_ref, b1_ref,
                       ex_ref, ov_ref, o_ref, msk_ref, oc_ref, ms_ref, as_ref, mw_ref, aw_ref,
                       *, n_tiles, n_cmp, n_blk):
    g = pl.program_id(1)
    t = pl.program_id(2)
    row = lax.broadcasted_iota(I32, (TQ, 1), 0)
    q_pos = t * TQ + row

    kc = kc_ref[0, 0]
    vc = vc_ref[0, 0]
    ncp = kc.shape[0]
    n_idx = lax.broadcasted_iota(I32, (1, ncp), 1)
    cmask = (CMP_STRIDE * n_idx + (CMP_LEN - 1) <= q_pos) & (n_idx < n_cmp)
    q_all = q_ref[0].reshape(NSA_G * TQ, HEAD_DIM)
    p = _masked_softmax(_dot_nt(q_all, kc).reshape(NSA_G, TQ, ncp), cmask[None])
    oc_ref[...] = _dot(p.reshape(NSA_G * TQ, ncp).astype(BF16), vc).reshape(NSA_G, TQ, HEAD_DIM)
    p_sum = jnp.sum(p, axis=0)
    hi, lo = _split_bf16(p_sum)
    imp = _dot_nt(ov_ref[...], hi) + _dot_nt(ov_ref[...], lo)

    blk = lax.broadcasted_iota(I32, (n_blk, 1), 0)
    cur = jnp.right_shift(t * TQ + lax.broadcasted_iota(I32, (1, TQ), 1), 6)
    visible = blk <= cur
    forced = (blk == 0) | (blk == cur) | (blk == cur - 1)
    score = jnp.where(visible, jnp.where(forced, FORCED, imp), -1.0)
    rank = jnp.zeros((n_blk, TQ), F32)
    for j in range(n_blk):
        other = score[j:j + 1, :]
        rank = rank + jnp.where((other > score) | ((other == score) & (j < blk)), 1.0, 0.0)
    sel = jnp.where(visible & (rank < float(N_SEL)), 1.0, 0.0).T
    mfull = _dot(sel.astype(BF16), ex_ref[...])
    for c in range(n_tiles):
        msk_ref[c] = mfull[:, c * TQ:(c + 1) * TQ]

    ms_ref[...] = jnp.full(ms_ref.shape, NEG, F32)
    mw_ref[...] = jnp.full(mw_ref.shape, NEG, F32)
    as_ref[...] = jnp.zeros(as_ref.shape, F32)
    aw_ref[...] = jnp.zeros(aw_ref.shape, F32)
    head = lax.broadcasted_iota(I32, (NSA_G, 1, 1), 0)
    far = jnp.zeros((NSA_G, 1, 1), F32)
    for h in range(NSA_G):
        far = jnp.where(head == h, far_ref[g * NSA_G + h], far)

    def sel_chunk(c, bias):
        mk = msk_ref[c] > 0.5
        _online_heads(q_all, ks_ref[0, 0, c], vs_ref[0, 0, c], jnp.where(mk[None], bias, NEG), ms_ref, as_ref)

    def far_body(c, carry):
        sel_chunk(c, far)
        return carry

    lax.fori_loop(0, jnp.maximum(t - 1, 0), far_body, 0)

    @pl.when(t >= 1)
    def _():
        sel_chunk(t - 1, b1_ref[...])

    sel_chunk(t, b0_ref[...])

    def win_chunk(c, add):
        _online_heads(q_all, kw_ref[0, 0, c], vw_ref[0, 0, c], add, mw_ref, aw_ref)

    win_chunk(t, b0_ref[...])

    @pl.when(t >= 1)
    def _():
        win_chunk(t - 1, b1_ref[...])

    @pl.when(t >= 2)
    def _():
        inside = lax.broadcasted_iota(I32, (TQ, TQ), 1) > lax.broadcasted_iota(I32, (TQ, TQ), 0)
        win_chunk(t - 2, jnp.where(inside[None], far, NEG))

    gates = jax.nn.sigmoid(g_ref[0, 0])
    for h in range(NSA_G):
        a_s, a_w = as_ref[h], aw_ref[h]
        o_s = a_s[:, :HEAD_DIM] * (1.0 / jnp.maximum(a_s[:, HEAD_DIM:HEAD_DIM + 1], TINY))
        o_w = a_w[:, :HEAD_DIM] * (1.0 / jnp.maximum(a_w[:, HEAD_DIM:HEAD_DIM + 1], TINY))
        o = (gates[:, 3 * h:3 * h + 1] * oc_ref[h] + gates[:, 3 * h + 1:3 * h + 2] * o_s
             + gates[:, 3 * h + 2:3 * h + 3] * o_w)
        o_ref[0, h] = o.astype(o_ref.dtype)


def _nsa_prompt_attention(q_hm, kc, vc, ks, vs, kw, vw, gates, b0, b1, far, expand, overlap):
    b, _, l, hd = q_hm.shape
    n_tiles = l // TQ
    ncp = kc.shape[2]
    n_blk = l // SEL_BLOCK
    k_spec = pl.BlockSpec((1, 1, n_tiles, TQ, hd), lambda bi, g, t, far: (bi, g, 0, 0, 0))
    v_spec = pl.BlockSpec((1, 1, n_tiles, TQ, 2 * hd), lambda bi, g, t, far: (bi, g, 0, 0, 0))
    cmp_spec = pl.BlockSpec((1, 1, ncp, hd), lambda bi, g, t, far: (bi, g, 0, 0))
    bias_spec = pl.BlockSpec((NSA_G, TQ, TQ), lambda bi, g, t, far: (g, 0, 0))
    kern = functools.partial(_nsa_prompt_kernel, n_tiles=n_tiles, n_cmp=ncp - 1, n_blk=n_blk)
    state = lambda: pltpu.VMEM((NSA_G, TQ, LANE), F32)
    return pl.pallas_call(
        kern,
        grid_spec=pltpu.PrefetchScalarGridSpec(
            num_scalar_prefetch=1,
            grid=(b, NSA_KV, n_tiles),
            in_specs=[pl.BlockSpec((1, NSA_G, TQ, hd), lambda bi, g, t, far: (bi, g, t, 0)),
                      cmp_spec, cmp_spec, k_spec, v_spec, k_spec, v_spec,
                      pl.BlockSpec((1, 1, TQ, 3 * NSA_G), lambda bi, g, t, far: (bi, g, t, 0)),
                      bias_spec, bias_spec,
                      pl.BlockSpec((n_blk, l), lambda bi, g, t, far: (0, 0)),
                      pl.BlockSpec((n_blk, ncp), lambda bi, g, t, far: (0, 0))],
            out_specs=pl.BlockSpec((1, NSA_G, TQ, hd), lambda bi, g, t, far: (bi, g, t, 0)),
            scratch_shapes=[pltpu.VMEM((n_tiles, TQ, TQ), F32), pltpu.VMEM((NSA_G, TQ, hd), F32),
                            state(), state(), state(), state()]),
        out_shape=jax.ShapeDtypeStruct(q_hm.shape, BF16),
        compiler_params=_cparams(("parallel", "parallel", "arbitrary")),
        name="nsa_prompt_attention",
    )(far, q_hm, kc, vc, ks, vs, kw, vw, gates, b0, b1, expand, overlap)


def _swa_prompt_kernel(sink_ref, q_ref, k_ref, v_ref, b0_ref, b1_ref, o_ref, m_ref, a_ref):
    g = pl.program_id(1)
    t = pl.program_id(2)
    lane = lax.broadcasted_iota(I32, (TQ, LANE), 1)
    for h in range(SWA_G):
        m_ref[h] = jnp.full((TQ, LANE), sink_ref[g * SWA_G + h], F32)
        a_ref[h] = jnp.where(lane >= HEAD_DIM, 1.0, 0.0)
    q_all = q_ref[0].reshape(SWA_G * TQ, HEAD_DIM)

    def chunk(c, bias_ref):
        _online_heads(q_all, k_ref[0, 0, c], v_ref[0, 0, c], bias_ref[...], m_ref, a_ref)

    chunk(t, b0_ref)

    @pl.when(t >= 1)
    def _():
        chunk(t - 1, b1_ref)

    for h in range(SWA_G):
        a = a_ref[h]
        o_ref[0, h] = (a[:, :HEAD_DIM] * (1.0 / a[:, HEAD_DIM:HEAD_DIM + 1])).astype(o_ref.dtype)


def _swa_prompt_attention(q_hm, k, v, b0, b1, sinks):
    b, _, l, hd = q_hm.shape
    n_tiles = l // TQ
    k_spec = pl.BlockSpec((1, 1, n_tiles, TQ, hd), lambda bi, g, t, s: (bi, g, 0, 0, 0))
    v_spec = pl.BlockSpec((1, 1, n_tiles, TQ, 2 * hd), lambda bi, g, t, s: (bi, g, 0, 0, 0))
    bias_spec = pl.BlockSpec((SWA_G, TQ, TQ), lambda bi, g, t, s: (g, 0, 0))
    return pl.pallas_call(
        _swa_prompt_kernel,
        grid_spec=pltpu.PrefetchScalarGridSpec(
            num_scalar_prefetch=1,
            grid=(b, SWA_KV, n_tiles),
            in_specs=[pl.BlockSpec((1, SWA_G, TQ, hd), lambda bi, g, t, s: (bi, g, t, 0)),
                      k_spec, v_spec, bias_spec, bias_spec],
            out_specs=pl.BlockSpec((1, SWA_G, TQ, hd), lambda bi, g, t, s: (bi, g, t, 0)),
            scratch_shapes=[pltpu.VMEM((SWA_G, TQ, LANE), F32), pltpu.VMEM((SWA_G, TQ, LANE), F32)]),
        out_shape=jax.ShapeDtypeStruct(q_hm.shape, BF16),
        compiler_params=_cparams(("parallel", "parallel", "arbitrary")),
        name="swa_prompt_attention",
    )(sinks, q_hm, k, v, b0, b1)


def _compress_kernel(pt_ref, *refs, npg, n_steps, feat_major):
    page_refs = refs[:npg]
    wbd_ref, b1_ref, w2_ref, kc_ref, vc_ref, x_s, tp_s = refs[npg:]
    s = pl.program_id(1)
    cpp = PAGE // CMP_STRIDE
    n_q = x_s.shape[1]
    for ip in range(npg // 2):
        base = pl.multiple_of((s * npg + 2 * ip) * cpp, 2 * cpp)
        if feat_major:
            for pi in range(2):
                for c in range(n_q):
                    tp_s[pi, c] = page_refs[2 * ip + pi][c * LANE:(c + 1) * LANE, :].T
        for j in range(CMP_STRIDE):
            for c in range(n_q):
                if feat_major:
                    halves = [tp_s[pi, c, pl.ds(j, cpp, stride=CMP_STRIDE), :] for pi in range(2)]
                else:
                    halves = [page_refs[2 * ip + pi][pl.ds(j * n_q + c, cpp, stride=CMP_STRIDE * n_q), :]
                              for pi in range(2)]
                x_s[j, c, pl.ds(base, 2 * cpp), :] = jnp.concatenate(halves, axis=0).astype(BF16)

    @pl.when(s == n_steps - 1)
    def _():
        n_chunks = x_s.shape[2]
        rows = lax.broadcasted_iota(I32, (n_chunks, 1), 0)
        for kv, out_ref in ((0, kc_ref), (1, vc_ref)):
            for gp in range(NSA_KV // 2):
                c = kv * (NSA_KV // 2) + gp
                y = jnp.zeros((n_chunks, 2 * LANE), F32)
                for jj in range(CMP_STRIDE // 2):
                    lhs = jnp.concatenate([x_s[2 * jj, c], x_s[2 * jj + 1, c]], axis=1)
                    y = y + _dot(lhs, wbd_ref[kv, jj])
                pre = y[:, :LANE] + pltpu.roll(y[:, LANE:], n_chunks - 1, 0) + b1_ref[kv]
                hid = jax.nn.gelu(pre)
                out = _dot(hid.astype(BF16), w2_ref[kv])
                out = jnp.where(rows < n_chunks - 1, out, 0.0)
                out_ref[0, :, gp * LANE:(gp + 1) * LANE] = out.astype(out_ref.dtype)


def _compress(pages, a, page_table, wbd, b1t, w2bd, feat_major):
    nseq, n_pages = page_table.shape
    npg = min(NPG_MAX, n_pages)
    n_steps = n_pages // npg
    n_chunks = n_pages * PAGE // CMP_STRIDE
    blk = pages.shape[2:]
    n_q = 2 * NSA_KV * HEAD_DIM // LANE

    def page_spec(i):
        return pl.BlockSpec((None, None) + blk, lambda b, s, pt: (a, pt[b, s * npg + i], 0, 0))

    out_spec = pl.BlockSpec((1, n_chunks, NSA_KV * HEAD_DIM), lambda b, s, pt: (b, 0, 0))
    kern = functools.partial(_compress_kernel, npg=npg, n_steps=n_steps, feat_major=feat_major)
    fix3 = lambda b, s, pt: (0, 0, 0)
    return pl.pallas_call(
        kern,
        grid_spec=pltpu.PrefetchScalarGridSpec(
            num_scalar_prefetch=1,
            grid=(nseq, n_steps),
            in_specs=[page_spec(i) for i in range(npg)] + [
                pl.BlockSpec(wbd.shape, lambda b, s, pt: (0, 0, 0, 0)),
                pl.BlockSpec(b1t.shape, fix3), pl.BlockSpec(w2bd.shape, fix3)],
            out_specs=[out_spec, out_spec],
            scratch_shapes=[pltpu.VMEM((CMP_STRIDE, n_q, n_chunks, LANE), BF16),
                            pltpu.VMEM((2, n_q, PAGE, LANE), F32)]),
        out_shape=[jax.ShapeDtypeStruct((nseq, n_chunks, NSA_KV * HEAD_DIM), BF16)] * 2,
        compiler_params=_cparams(("parallel", "arbitrary")),
        name="compress",
    )(page_table, *([pages] * npg), wbd, b1t, w2bd)


def _diag_blocks(o, n_groups, rows_per_group):
    return jnp.concatenate(
        [o[g * rows_per_group:(g + 1) * rows_per_group, g * HEAD_DIM:(g + 1) * HEAD_DIM] for g in range(n_groups)],
        axis=0)


def _nsa_sample_kernel(pt_ref, *refs, npg, n_steps, n_cmp, n_blk):
    page_refs = refs[:npg]
    (q_ref, kc_ref, vc_ref, tails_ref, tailw_ref, win_ref, g_ref, farc_ref, blast_ref, btail_ref, bwin_ref,
     cpos_ref, hsum_ref, ov_ref, o_ref, msk_ref, tsel_ref, oc_ref, m_ref, l_ref, a_ref) = refs[npg:]
    s = pl.program_id(1)
    q = q_ref[0]
    n_rows = q.shape[0]
    gw = NSA_KV * HEAD_DIM
    bps = 2 * npg

    @pl.when(s == 0)
    def _():
        kc = kc_ref[0]
        ncp = kc.shape[0]
        n_idx = lax.broadcasted_iota(I32, (1, ncp), 1)
        cmask = (CMP_STRIDE * n_idx + (CMP_LEN - 1) <= cpos_ref[...]) & (n_idx < n_cmp)
        p = _masked_softmax(_dot_nt(q, kc), cmask)
        oc_ref[...] = _dot(p.astype(BF16), vc_ref[0])
        hi, lo = _split_bf16(p)
        p_grp = _dot(hsum_ref[...], hi) + _dot(hsum_ref[...], lo)
        hi, lo = _split_bf16(p_grp)
        imp = _dot(hi, ov_ref[...]) + _dot(lo, ov_ref[...])
        nbp = imp.shape[1]
        blk = lax.broadcasted_iota(I32, (1, nbp), 1)
        cur = n_blk - 1
        visible = blk <= cur
        forced = (blk == 0) | (blk == cur) | (blk == cur - 1)
        score = jnp.where(visible, jnp.where(forced, FORCED, imp), -1.0)
        sel = jnp.where(visible & _topk_mask(score, blk, n_blk), 1.0, 0.0)
        for st in range(n_steps):
            msk_ref[st, :, 0:bps] = sel[:, st * bps:(st + 1) * bps]
        tsel_ref[...] = sel[:, n_blk - 1:n_blk]
        m_ref[...] = jnp.full((n_rows, 1), NEG, F32)
        l_ref[...] = jnp.zeros((n_rows, 1), F32)
        a_ref[...] = jnp.zeros((n_rows, gw), F32)

    ms = msk_ref[s]
    lane = lax.broadcasted_iota(I32, (1, PAGE), 1)
    farc = farc_ref[...]
    for i0 in range(0, npg, PAGES_PER_STEP):
        ks, vs, adds = [], [], []
        for i in range(i0, min(i0 + PAGES_PER_STEP, npg)):
            page = page_refs[i]
            ks.append(page[0:gw, :].astype(BF16))
            vs.append(page[gw:2 * gw, :].astype(BF16))
            mk = jnp.where(lane < SEL_BLOCK, ms[:, 2 * i:2 * i + 1], ms[:, 2 * i + 1:2 * i + 2]) > 0.5
            if i == npg - 1:
                bias = jnp.where(s == n_steps - 1, blast_ref[...], farc)
            else:
                bias = farc
            adds.append(jnp.where(mk, bias, NEG))
        m_ref[...], l_ref[...], a_ref[...] = _online_multi(
            q, ks, vs, adds, (m_ref[...], l_ref[...], a_ref[...]), feat_major=True)

    @pl.when(s == n_steps - 1)
    def _():
        tails = tails_ref[0]
        _online_step(q, tails[:, 0:gw], tails[:, gw:2 * gw], jnp.where(tsel_ref[...] > 0.5, btail_ref[...], NEG),
                     m_ref, l_ref, a_ref, slice(None))
        o_s = a_ref[...] * (1.0 / jnp.maximum(l_ref[...], TINY))

        tailw = tailw_ref[0]
        state = (jnp.full((n_rows, 1), NEG, F32), jnp.zeros((n_rows, 1), F32), jnp.zeros((n_rows, gw), F32))
        state = _online_value(q, tailw[:, 0:gw], tailw[:, gw:2 * gw], btail_ref[...], state)
        wl = win_ref.shape[3]
        ks, vs, adds = [], [], []
        for c in range(wl // PAGE):
            rows = win_ref[0, 0, :, c * PAGE:(c + 1) * PAGE]
            ks.append(rows[0:gw, :].astype(BF16))
            vs.append(rows[gw:2 * gw, :].astype(BF16))
            adds.append(bwin_ref[:, c * PAGE:(c + 1) * PAGE])
        state = _online_multi(q, ks, vs, adds, state, feat_major=True)
        o_w = state[2] * (1.0 / jnp.maximum(state[1], TINY))

        gates = jax.nn.sigmoid(g_ref[0])
        o = gates[:, 0:1] * oc_ref[...] + gates[:, 1:2] * o_s + gates[:, 2:3] * o_w
        o_ref[0] = _diag_blocks(o, NSA_KV, n_rows // NSA_KV)


def _nsa_sample_attention(q_bd, kc, vc, cache4, a, page_table, tail_s, tail_w, win4, gates, farc, blast, btail, bwin,
                          cpos, hsum, overlap, n_blk):
    bs, n_rows, gw = q_bd.shape
    n_pages = page_table.shape[1]
    npg = min(NPG_MAX, n_pages)
    n_steps = n_pages // npg
    ncp = kc.shape[1]
    wl = win4.shape[3]

    def page_spec(i):
        return pl.BlockSpec((None, None, 2 * gw, PAGE), lambda b, s, pt: (a, pt[b, s * npg + i], 0, 0))

    per_b3 = lambda b, s, pt: (b, 0, 0)
    fix2 = lambda b, s, pt: (0, 0)
    full2 = lambda arr: pl.BlockSpec(arr.shape, fix2)
    kern = functools.partial(_nsa_sample_kernel, npg=npg, n_steps=n_steps, n_cmp=ncp - 1, n_blk=n_blk)
    return pl.pallas_call(
        kern,
        grid_spec=pltpu.PrefetchScalarGridSpec(
            num_scalar_prefetch=1,
            grid=(bs, n_steps),
            in_specs=[page_spec(i) for i in range(npg)] + [
                pl.BlockSpec((1, n_rows, gw), per_b3),
                pl.BlockSpec((1, ncp, gw), per_b3), pl.BlockSpec((1, ncp, gw), per_b3),
                pl.BlockSpec((1, PAGE, 2 * gw), per_b3), pl.BlockSpec((1, PAGE, 2 * gw), per_b3),
                pl.BlockSpec((1, 1, 2 * gw, wl), lambda b, s, pt: (a, b, 0, 0)),
                pl.BlockSpec((1, n_rows, 3), per_b3),
                full2(farc), full2(blast), full2(btail), full2(bwin), full2(cpos), full2(hsum), full2(overlap)],
            out_specs=pl.BlockSpec((1, n_rows, HEAD_DIM), per_b3),
            scratch_shapes=[pltpu.VMEM((n_steps, n_rows, LANE), F32), pltpu.VMEM((n_rows, 1), F32),
                            pltpu.VMEM((n_rows, gw), F32), pltpu.VMEM((n_rows, 1), F32),
                            pltpu.VMEM((n_rows, 1), F32), pltpu.VMEM((n_rows, gw), F32)]),
        out_shape=jax.ShapeDtypeStruct((bs, n_rows, HEAD_DIM), F32),
        compiler_params=_cparams(("parallel", "arbitrary")),
        name="nsa_sample_attention",
    )(page_table, *([cache4] * npg), q_bd, kc, vc, tail_s, tail_w, win4, gates, farc, blast, btail, bwin,
      cpos, hsum, overlap)


def _swa_sample_kernel(q_ref, buf_ref, tail_ref, sink_ref, bbuf_ref, btail_ref, o_ref):
    q = q_ref[0]
    n_rows = q.shape[0]
    gw = SWA_KV * HEAD_DIM
    state = (sink_ref[...], jnp.ones((n_rows, 1), F32), jnp.zeros((n_rows, gw), F32))
    tail = tail_ref[0]
    state = _online_value(q, tail[:, 0:gw], tail[:, gw:2 * gw], btail_ref[...], state)
    wl = buf_ref.shape[3]
    for c in range(wl // PAGE):
        rows = buf_ref[0, 0, :, c * PAGE:(c + 1) * PAGE]
        state = _online_value(q, rows[0:gw, :].astype(BF16), rows[gw:2 * gw, :].astype(BF16),
                              bbuf_ref[:, c * PAGE:(c + 1) * PAGE], state, feat_major=True)
    o = state[2] * (1.0 / state[1])
    o_ref[0] = _diag_blocks(o, SWA_KV, n_rows // SWA_KV)


def _swa_sample_attention(q_bd, buf4, a, tail, sinkc, bbuf, btail):
    bs, n_rows, gw = q_bd.shape
    wl = buf4.shape[3]
    per_b3 = lambda b: (b, 0, 0)
    full2 = lambda arr: pl.BlockSpec(arr.shape, lambda b: (0, 0))
    return pl.pallas_call(
        _swa_sample_kernel,
        grid=(bs,),
        in_specs=[pl.BlockSpec((1, n_rows, gw), per_b3),
                  pl.BlockSpec((1, 1, 2 * gw, wl), lambda b: (a, b, 0, 0)),
                  pl.BlockSpec((1, PAGE, 2 * gw), per_b3),
                  full2(sinkc), full2(bbuf), full2(btail)],
        out_specs=pl.BlockSpec((1, n_rows, HEAD_DIM), per_b3),
        out_shape=jax.ShapeDtypeStruct((bs, n_rows, HEAD_DIM), F32),
        compiler_params=_cparams(("parallel",)),
        name="swa_sample_attention",
    )(q_bd, buf4, tail, sinkc, bbuf, btail)


def _rel_bucket(dist):
    n = jnp.maximum(dist, 0)
    nf = jnp.maximum(n, 1).astype(F32)
    far = REL_MAX_EXACT + (jnp.log(nf / REL_MAX_EXACT) / math.log(REL_MAX_DIST / REL_MAX_EXACT)
                           * (REL_BUCKETS - REL_MAX_EXACT)).astype(I32)
    return jnp.where(n < REL_MAX_EXACT, n, jnp.minimum(far, REL_BUCKETS - 1))


def _bias_of(rel_bias, dist):
    return jnp.moveaxis(rel_bias[_rel_bucket(dist)], -1, 0)


def _prompt_bias_tiles(rel_bias, window):
    qi = jnp.arange(TQ)[:, None]
    kj = jnp.arange(TQ)[None, :]
    d0 = qi - kj
    d1 = TQ + qi - kj
    b0 = jnp.where((d0 >= 0) & (d0 < window), _bias_of(rel_bias, d0), NEG)
    b1 = jnp.where(d1 < window, _bias_of(rel_bias, d1), NEG)
    return b0.astype(F32), b1.astype(F32)


def _sample_bias_tiles(rel_bias, t, wl, window):
    n_rows = N_HEADS * t
    head = jnp.arange(n_rows) // t
    q = (jnp.arange(n_rows) % t)[:, None]
    kj = jnp.arange(PAGE)[None, :]
    d_tail = q - kj
    tail_all = _bias_of(rel_bias, d_tail)[head, jnp.arange(n_rows)]
    btail = jnp.where((d_tail >= 0) & (kj < t), tail_all, NEG)
    i = jnp.arange(wl)[None, :]
    d_buf = wl + q - i
    buf_all = _bias_of(rel_bias, d_buf)[head, jnp.arange(n_rows)]
    bbuf = jnp.where(d_buf < window, buf_all, NEG)
    return btail.astype(F32), bbuf.astype(F32), head, q


def _block_diag_q(q, n_kv, t):
    bs = q.shape[0]
    grp = N_HEADS // n_kv
    q5 = q.reshape(bs, t, n_kv, grp, HEAD_DIM).transpose(0, 2, 3, 1, 4)
    eye = jnp.eye(n_kv, dtype=q.dtype)
    qbd = jnp.einsum('bghqd,gk->bghqkd', q5, eye)
    return qbd.reshape(bs, N_HEADS * t, n_kv * HEAD_DIM).astype(BF16)


def _rows_to_tokens(o, t):
    bs = o.shape[0]
    return o.reshape(bs, N_HEADS, t, HEAD_DIM).transpose(0, 2, 1, 3).reshape(bs * t, N_HEADS * HEAD_DIM)


def _head_major(x, n_heads):
    b, l, _ = x.shape
    return x.reshape(b, l, n_heads, HEAD_DIM).transpose(0, 2, 1, 3).astype(BF16)


def _with_ones(v):
    return jnp.concatenate([v, jnp.ones_like(v)], axis=-1)


def _pad_tail(rows, t):
    return jnp.pad(rows, ((0, 0), (0, PAGE - t), (0, 0))).astype(BF16)


def _compress_weights(w1, b1, w2):
    w = w1.reshape(2, 2, CMP_STRIDE // 2, 2, HEAD_DIM, HEAD_DIM)
    eye = jnp.eye(2, dtype=w1.dtype)
    wbd = jnp.einsum('kpjlde,gh->kjlgdphe', w, eye).reshape(2, CMP_STRIDE // 2, 4 * HEAD_DIM, 4 * HEAD_DIM)
    b1t = jnp.tile(b1, (1, 2)).reshape(2, 1, 2 * HEAD_DIM)
    w2bd = jnp.einsum('kde,gh->kgdhe', w2, eye).reshape(2, 2 * HEAD_DIM, 2 * HEAD_DIM)
    return wbd.astype(BF16), b1t.astype(F32), w2bd.astype(BF16)


def _overlap(n_cmp_rows, n_blk_cols):
    n = CMP_STRIDE * jnp.arange(n_cmp_rows)[:, None]
    j = jnp.arange(n_blk_cols)[None, :]
    return ((n < SEL_BLOCK * (j + 1)) & (n + CMP_LEN > SEL_BLOCK * j)).astype(BF16)


def _dispatch_plan(ids):
    n = ids.shape[0]
    order = jnp.argsort(ids)
    sid = ids[order]
    counts = jnp.bincount(ids, length=N_EXPERTS)
    padded = (counts + TM - 1) // TM * TM
    pad_end = jnp.cumsum(padded)
    pad_start = pad_end - padded
    raw_start = jnp.cumsum(counts) - counts
    dest = (pad_start[sid] + jnp.arange(n) - raw_start[sid]).astype(I32)
    n_blocks = -(-n // TM) + N_EXPERTS
    src = jnp.zeros((n_blocks * TM,), I32).at[dest].set((order // TOP_K).astype(I32))
    blk_start = jnp.arange(n_blocks) * TM
    blk_expert = jnp.minimum(jnp.sum(pad_end[None, :] <= blk_start[:, None], axis=1), N_EXPERTS - 1).astype(I32)
    n_used = (pad_end[-1:] // TM).astype(I32)
    pos = jnp.zeros((n,), I32).at[order].set(dest)
    return src, blk_expert, n_used, pos


def _moe(hn, ids, w, w_gate, w_up, w_down, layer, g, b):
    src, blk_expert, n_used, pos = _dispatch_plan(ids.reshape(-1))
    out = _experts(hn[src], blk_expert, n_used, w_gate, w_up, w_down, layer)
    return _combine_ln(hn, out[pos[0::2]], out[pos[1::2]], w, g, b)


def kernel(x_prompt, x_sample, cache_nsa_cmp, cache_nsa_sel, state_nsa_win, state_swa_win, page_table, rel_bias, nsa_w_in, nsa_b_in, nsa_cmp_w1, nsa_cmp_b1, nsa_cmp_w2, nsa_w_out, swa_w_in, swa_sinks, swa_w_out, moe_w_group, moe_b_group, moe_w_expert, moe_b_expert, moe_w_gate, moe_w_up, moe_w_down, ln_g, ln_b):
    bp, l, d = x_prompt.shape
    bs, t, _ = x_sample.shape
    n_pages = page_table.shape[1]
    past = n_pages * PAGE
    ntp = bp * l
    hq = N_HEADS * HEAD_DIM
    gw_n = NSA_KV * HEAD_DIM
    gw_s = SWA_KV * HEAD_DIM
    assert l % TQ == 0 and NSA_WINDOW == 2 * TQ and SWA_WINDOW <= TQ and TQ >= REL_MAX_DIST
    assert ntp % TM == 0 and t <= CMP_LEN - 1 and t <= SEL_BLOCK and n_pages % 2 == 0
    nsa_wl = state_nsa_win.shape[2]
    swa_wl = state_swa_win.shape[2]
    assert nsa_wl % PAGE == 0 and swa_wl % PAGE == 0
    nts = bs * t
    row_pad = -(ntp + nts) % TM

    x = jnp.concatenate([x_prompt.reshape(ntp, d), x_sample.reshape(nts, d), jnp.zeros((row_pad, d), F32)], axis=0)

    far = rel_bias[REL_BUCKETS - 1].astype(F32)
    nb0, nb1 = _prompt_bias_tiles(rel_bias, NSA_WINDOW)
    sb0, sb1 = _prompt_bias_tiles(rel_bias, SWA_WINDOW)
    n_blk_p = l // SEL_BLOCK
    expand = (jnp.arange(l)[None, :] // SEL_BLOCK == jnp.arange(n_blk_p)[:, None]).astype(BF16)
    ov_p = _overlap(l // CMP_STRIDE, n_blk_p)
    n_blk_s = past // SEL_BLOCK + 1
    nbp_s = -(-n_blk_s // LANE) * LANE
    ov_s = _overlap(past // CMP_STRIDE, nbp_s)
    n_rows = N_HEADS * t
    btail, nbwin, head, qrow = _sample_bias_tiles(rel_bias, t, nsa_wl, NSA_WINDOW)
    _, sbbuf, _, _ = _sample_bias_tiles(rel_bias, t, swa_wl, SWA_WINDOW)
    farc = far[head][:, None]
    d_last = PAGE + qrow - jnp.arange(PAGE)[None, :]
    blast = _bias_of(rel_bias, d_last)[head, jnp.arange(n_rows)].astype(F32)
    cpos = (past + qrow).astype(I32)
    grp_rows = NSA_G * t
    r = jnp.arange(n_rows)
    hsum = ((r[:, None] // grp_rows == r[None, :] // grp_rows) & (r[:, None] % t == r[None, :] % t)).astype(BF16)
    prompt_pages = jnp.arange(ntp // PAGE, dtype=I32).reshape(bp, l // PAGE)

    def feat_major(z):
        return jnp.transpose(z, (0, 1, 3, 4, 5, 2)).reshape(z.shape[0], z.shape[1], -1, z.shape[2])

    cache_cmp4 = feat_major(cache_nsa_cmp)
    cache_sel4 = feat_major(cache_nsa_sel)
    nsa_win4 = feat_major(state_nsa_win)
    swa_win4 = feat_major(state_swa_win)

    outs = {k: [] for k in ('cmp_p', 'cmp_s', 'sel_p', 'sel_s', 'nwin_p', 'nwin_s', 'swin_p', 'swin_s')}
    for layer in range(DEPTH):
        j = layer // 2
        if layer % 2 == 0:
            nsa_in = nsa_w_in.shape[-1]
            n_pad = -(-nsa_in // LANE) * LANE
            w_in = jnp.pad(nsa_w_in[j], ((0, 0), (0, n_pad - nsa_in))).astype(BF16)
            b_in = jnp.pad(nsa_b_in[j], (0, n_pad - nsa_in)).reshape(1, n_pad)
            h = _project(x, w_in, b_in)
            hp = h[:ntp].reshape(bp, l, n_pad)
            hs = h[ntp:ntp + nts].reshape(bs, t, n_pad)
            c0, c1, c2, c3 = hq, hq + 2 * gw_n, hq + 4 * gw_n, hq + 6 * gw_n
            wbd, b1t, w2bd = _compress_weights(nsa_cmp_w1[j], nsa_cmp_b1[j], nsa_cmp_w2[j])

            kvc, kvs, kvw = hp[..., c0:c1], hp[..., c1:c2], hp[..., c2:c3]
            kc, vc = _compress(kvc.reshape(1, ntp // PAGE, PAGE * 2 * gw_n // LANE, LANE), 0, prompt_pages,
                               wbd, b1t, w2bd, feat_major=False)
            chunked = lambda z: _head_major(z, NSA_KV).reshape(bp, NSA_KV, l // TQ, TQ, HEAD_DIM)
            gates_p = hp[..., c3:c3 + 3 * N_HEADS].reshape(bp, l, NSA_KV, 3 * NSA_G).transpose(0, 2, 1, 3)
            o_p = _nsa_prompt_attention(
                _head_major(hp[..., :hq] * SCALE, N_HEADS), _head_major(kc, NSA_KV), _head_major(vc, NSA_KV),
                chunked(kvs[..., :gw_n]), _with_ones(chunked(kvs[..., gw_n:])),
                chunked(kvw[..., :gw_n]), _with_ones(chunked(kvw[..., gw_n:])),
                gates_p, nb0, nb1, far, expand, ov_p.T)
            o_p = o_p.transpose(0, 2, 1, 3).reshape(ntp, hq)
            kv_shape = (2, NSA_KV, HEAD_DIM)
            outs['cmp_p'].append(kvc.reshape(bp, l, *kv_shape))
            outs['sel_p'].append(kvs.reshape(bp, l, *kv_shape))
            wl_p = min(NSA_WINDOW, l)
            outs['nwin_p'].append(kvw[:, l - wl_p:].reshape(bp, wl_p, *kv_shape))

            kcs, vcs = _compress(cache_cmp4, j, page_table, wbd, b1t, w2bd, feat_major=True)
            gates_s = hs[..., c3:c3 + 3 * N_HEADS].reshape(bs, t, N_HEADS, 3).transpose(0, 2, 1, 3)
            o_s = _nsa_sample_attention(
                _block_diag_q(hs[..., :hq] * SCALE, NSA_KV, t), kcs, vcs, cache_sel4, j, page_table,
                _pad_tail(hs[..., c1:c2], t), _pad_tail(hs[..., c2:c3], t), nsa_win4,
                gates_s.reshape(bs, n_rows, 3), farc, blast, btail, nbwin, cpos, hsum, ov_s, n_blk_s)
            outs['cmp_s'].append(hs[..., c0:c1].reshape(bs, t, *kv_shape))
            outs['sel_s'].append(hs[..., c1:c2].reshape(bs, t, *kv_shape))
            outs['nwin_s'].append(hs[..., c2:c3].reshape(bs, t, *kv_shape))
            w_out = nsa_w_out[j]
        else:
            swa_in = swa_w_in.shape[-1]
            h = _project(x, swa_w_in[j].astype(BF16), jnp.zeros((1, swa_in), F32))
            hp = h[:ntp].reshape(bp, l, swa_in)
            hs = h[ntp:ntp + nts].reshape(bs, t, swa_in)
            kp, vp = hp[..., hq:hq + gw_s], hp[..., hq + gw_s:]
            chunked = lambda z: _head_major(z, SWA_KV).reshape(bp, SWA_KV, l // TQ, TQ, HEAD_DIM)
            o_p = _swa_prompt_attention(_head_major(hp[..., :hq] * SCALE, N_HEADS), chunked(kp),
                                        _with_ones(chunked(vp)),
                                        sb0, sb1, swa_sinks[j].astype(F32))
            o_p = o_p.transpose(0, 2, 1, 3).reshape(ntp, hq)
            wl_p = min(SWA_WINDOW, l)
            sw_shape = (SWA_KV, HEAD_DIM)
            outs['swin_p'].append(jnp.stack([kp[:, l - wl_p:].reshape(bp, wl_p, *sw_shape),
                                             vp[:, l - wl_p:].reshape(bp, wl_p, *sw_shape)], axis=2))
            o_s = _swa_sample_attention(_block_diag_q(hs[..., :hq] * SCALE, SWA_KV, t), swa_win4, j,
                                        _pad_tail(hs[..., hq:], t), swa_sinks[j][head][:, None].astype(F32),
                                        sbbuf, btail)
            new_rows = jnp.stack([hs[..., hq:hq + gw_s].reshape(bs, t, *sw_shape),
                                  hs[..., hq + gw_s:].reshape(bs, t, *sw_shape)], axis=2)
            outs['swin_s'].append(new_rows)
            w_out = swa_w_out[j]

        o = jnp.concatenate([o_p, _rows_to_tokens(o_s, t).astype(BF16), jnp.zeros((row_pad, hq), BF16)], axis=0)
        wr = jnp.pad(jnp.concatenate([moe_w_group[layer], moe_w_expert[layer]], axis=1),
                     ((0, 0), (0, ROUTER_PAD - N_GROUPS - N_EXPERTS)))
        br = jnp.pad(jnp.concatenate([moe_b_group[layer], moe_b_expert[layer]]),
                     (0, ROUTER_PAD - N_GROUPS - N_EXPERTS)).reshape(1, ROUTER_PAD)
        wr_hi, wr_lo = _split_bf16(wr)
        hn, ids, wts = _outproj_ln_router(o, w_out.astype(BF16), x, ln_g[layer, 0].reshape(1, d),
                                          ln_b[layer, 0].reshape(1, d), wr_hi, wr_lo, br)
        x = _moe(hn, ids, wts, moe_w_gate, moe_w_up, moe_w_down, layer,
                 ln_g[layer, 1].reshape(1, d), ln_b[layer, 1].reshape(1, d))

    st = lambda k: jnp.stack(outs[k])
    nwin_s = jnp.concatenate([state_nsa_win[:, :, t:], st('nwin_s')], axis=2)
    swin_s = jnp.concatenate([state_swa_win[:, :, t:], st('swin_s')], axis=2)
    return (x[:ntp].reshape(bp, l, d), x[ntp:ntp + nts].reshape(bs, t, d), st('cmp_p'), st('cmp_s'), st('sel_p'), st('sel_s'),
            st('nwin_p'), nwin_s, st('swin_p'), swin_s)
```
